```python
import jax, jax.numpy as jnp
from jax import lax
import numpy as np

D_MODEL = 1024
BATCH = 32
SEQ = 2048
DEPTH = 2

CHUNK = 64
HEAD_DIM = 64
ROPE_THETA = 10000.0
EPS = 1e-6
A_Q_HEADS = (D_MODEL // 2) // HEAD_DIM
A_KV_HEADS = A_Q_HEADS // 4
A_WINDOW = 128
A_PREV_CHUNKS = A_WINDOW // CHUNK
B_KEY_DIM = 128
B_VAL_DIM = 128
B_HEADS = (D_MODEL // 2) // B_VAL_DIM
B_BLOCK = 32
C_HEADS = D_MODEL // HEAD_DIM
C_PREV_CHUNKS = 8
REL_CLIP = 128
D_FF = 2816
CONV_WIDTH = 3
A_Q_DIM = A_Q_HEADS * HEAD_DIM
A_KV_DIM = A_KV_HEADS * HEAD_DIM
B_QK_DIM = B_HEADS * B_KEY_DIM
B_V_DIM = B_HEADS * B_VAL_DIM
EVEN_IN = A_Q_DIM + 2 * A_KV_DIM + 2 * B_QK_DIM + 2 * B_V_DIM
EVEN_MIX = A_Q_DIM + B_V_DIM
EVEN_SPLITS = (A_Q_DIM,
               A_Q_DIM + A_KV_DIM,
               A_Q_DIM + 2 * A_KV_DIM,
               A_Q_DIM + 2 * A_KV_DIM + B_QK_DIM,
               A_Q_DIM + 2 * A_KV_DIM + 2 * B_QK_DIM,
               A_Q_DIM + 2 * A_KV_DIM + 2 * B_QK_DIM + B_V_DIM)
C_DIM = C_HEADS * HEAD_DIM
ODD_IN = 3 * C_DIM
N_EVEN = (DEPTH + 1) // 2
N_ODD = DEPTH // 2

kernel_name = "chunk_causal_hybrid_swa_hgrn2_relpos_convffn"

F32 = jnp.float32


def rms_norm(x, w):
    xf = x.astype(F32)
    y = xf * lax.rsqrt(jnp.mean(xf * xf, axis=-1, keepdims=True) + EPS)
    return (y * w.astype(F32)).astype(x.dtype)


def rope(x):
    S, D = x.shape[1], x.shape[3]
    inv_freq = ROPE_THETA ** (-jnp.arange(0, D, 2, dtype=F32) / D)
    ang = jnp.arange(S, dtype=F32)[:, None] * inv_freq[None, :]
    cos = jnp.cos(ang)[None, :, None, :]
    sin = jnp.sin(ang)[None, :, None, :]
    xf = x.astype(F32)
    x1, x2 = xf[..., : D // 2], xf[..., D // 2:]
    return jnp.concatenate([x1 * cos - x2 * sin, x2 * cos + x1 * sin], axis=-1).astype(x.dtype)


def chunk_band_attention(q, k, v, n_prev, bias, sinks):
    B, S, Hq, D = q.shape
    Hkv = k.shape[2]
    G = Hq // Hkv
    n_chunks = S // CHUNK
    pad = n_prev * CHUNK
    band = pad + CHUNK
    scale = D ** -0.5
    k_pad = jnp.pad(k, ((0, 0), (pad, 0), (0, 0), (0, 0)))
    v_pad = jnp.pad(v, ((0, 0), (pad, 0), (0, 0), (0, 0)))
    qc = jnp.moveaxis(q.reshape(B, n_chunks, CHUNK, Hkv, G, D), 1, 0)
    key_idx = jnp.arange(band)

    def one_chunk(args):
        c, q_c = args
        start = c * CHUNK
        k_c = lax.dynamic_slice_in_dim(k_pad, start, band, axis=1)
        v_c = lax.dynamic_slice_in_dim(v_pad, start, band, axis=1)
        s = jnp.einsum('bqhgd,bjhd->bhgqj', q_c, k_c).astype(F32) * scale
        if bias is not None:
            s = s + bias.reshape(Hkv, G, CHUNK, band).astype(F32)
        valid = key_idx >= pad - start
        s = jnp.where(valid, s, -jnp.inf)
        if sinks is not None:
            sk = sinks.reshape(Hkv, G).astype(F32)[None, :, :, None, None]
            m = jnp.maximum(jnp.max(s, axis=-1, keepdims=True), sk)
            p = jnp.exp(s - m)
            p = p / (jnp.sum(p, axis=-1, keepdims=True) + jnp.exp(sk - m))
        else:
            p = jax.nn.softmax(s, axis=-1)
        o = jnp.einsum('bhgqj,bjhd->bqhgd', p.astype(v.dtype), v_c)
        return o.reshape(B, CHUNK, Hq, D)

    out = lax.map(one_chunk, (jnp.arange(n_chunks), qc))
    return jnp.moveaxis(out, 0, 1).reshape(B, S, Hq, D)


def hgrn2(q, f_logit, i, g, lb, norm_w):
    B, S, H, DK = q.shape
    DV = i.shape[-1]
    N = S // B_BLOCK
    lbh = lb.reshape(H, DK).astype(F32)
    f = lbh + (1.0 - lbh) * jax.nn.sigmoid(f_logit.astype(F32))
    log_f = jnp.log(f)
    k = 1.0 - f

    def blk(t):
        return t.reshape(B, N, B_BLOCK, H, t.shape[-1])

    qf, kf, vf, lf = blk(q.astype(F32)), blk(k), blk(i.astype(F32)), blk(log_f)
    cum = jnp.cumsum(lf, axis=2)
    last = cum[:, :, -1:]
    q_dec = qf * jnp.exp(cum)
    k_dec = kf * jnp.exp(-cum)
    k_end = kf * jnp.exp(last - cum)
    blk_decay = jnp.exp(last[:, :, 0])
    causal = jnp.tril(jnp.ones((B_BLOCK, B_BLOCK), dtype=bool))
    s = jnp.einsum('bnlhd,bnmhd->bnhlm', q_dec, k_dec)
    s = jnp.where(causal, s, 0.0)
    o_intra = jnp.einsum('bnhlm,bnmhe->bnlhe', s, vf)

    def step(state, xs):
        qd, ke, vv, dec = xs
        o = jnp.einsum('blhd,bhde->blhe', qd, state)
        state = dec[..., None] * state + jnp.einsum('blhd,blhe->bhde', ke, vv)
        return state, o

    state0 = jnp.zeros((B, H, DK, DV), F32)
    xs = (jnp.moveaxis(q_dec, 1, 0), jnp.moveaxis(k_end, 1, 0),
          jnp.moveaxis(vf, 1, 0), jnp.moveaxis(blk_decay, 1, 0))
    _, o_inter = lax.scan(step, state0, xs)
    o = (o_intra + jnp.moveaxis(o_inter, 0, 1)).reshape(B, S, H, DV)
    o = o * lax.rsqrt(jnp.mean(o * o, axis=-1, keepdims=True) + EPS) * norm_w.reshape(H, DV).astype(F32)
    return (o * jax.nn.silu(g.astype(F32))).astype(i.dtype)


def even_mixer(y, w_in, w_out, sinks, lb, gnorm_w):
    B, S, _ = y.shape
    proj = y @ w_in
    qa, ka, va, qb, fb, ib, gb = jnp.split(proj, EVEN_SPLITS, axis=-1)
    qa = rope(qa.reshape(B, S, A_Q_HEADS, HEAD_DIM))
    ka = rope(ka.reshape(B, S, A_KV_HEADS, HEAD_DIM))
    va = va.reshape(B, S, A_KV_HEADS, HEAD_DIM)
    oa = chunk_band_attention(qa, ka, va, A_PREV_CHUNKS, None, sinks)
    ob = hgrn2(qb.reshape(B, S, B_HEADS, B_KEY_DIM), fb.reshape(B, S, B_HEADS, B_KEY_DIM),
               ib.reshape(B, S, B_HEADS, B_VAL_DIM), gb.reshape(B, S, B_HEADS, B_VAL_DIM),
               lb, gnorm_w)
    o = jnp.concatenate([oa.reshape(B, S, A_Q_DIM), ob.reshape(B, S, B_V_DIM)], axis=-1)
    return o @ w_out


def odd_mixer(y, w_in, w_out, rel_table):
    B, S, _ = y.shape
    proj = y @ w_in
    q, k, v = jnp.split(proj, (C_DIM, 2 * C_DIM), axis=-1)
    q = q.reshape(B, S, C_HEADS, HEAD_DIM)
    k = k.reshape(B, S, C_HEADS, HEAD_DIM)
    v = v.reshape(B, S, C_HEADS, HEAD_DIM)
    pad = C_PREV_CHUNKS * CHUNK
    band = pad + CHUNK
    rel = jnp.arange(CHUNK)[:, None] + pad - jnp.arange(band)[None, :]
    idx = jnp.clip(rel, -REL_CLIP, REL_CLIP) + REL_CLIP
    bias = rel_table[:, idx]
    o = chunk_band_attention(q, k, v, C_PREV_CHUNKS, bias, None)
    return o.reshape(B, S, C_DIM) @ w_out


def conv_ffn(y, w_up, conv_w, conv_b, w_down):
    S = y.shape[1]
    up = y @ w_up
    u, v = jnp.split(up, 2, axis=-1)
    u_pad = jnp.pad(u, ((0, 0), (CONV_WIDTH - 1, 0), (0, 0)))
    c = conv_b
    for j in range(CONV_WIDTH):
        c = c + conv_w[j] * u_pad[:, j:j + S]
    return (jax.nn.gelu(c, approximate=False) * v) @ w_down


def setup_inputs(seed: int = 0) -> dict:
    key = jax.random.key(seed)
    ks = jax.random.split(key, 16)
    nrm = jax.random.normal
    x = nrm(ks[0], (BATCH, SEQ, D_MODEL), F32)
    even_w_in = nrm(ks[1], (N_EVEN, D_MODEL, EVEN_IN), F32) * D_MODEL ** -0.5
    even_w_out = nrm(ks[2], (N_EVEN, EVEN_MIX, D_MODEL), F32) * EVEN_MIX ** -0.5
    even_sinks = nrm(ks[3], (N_EVEN, A_Q_HEADS), F32) * 0.5
    hgrn_lb_logits = nrm(ks[4], (N_EVEN + 1, B_QK_DIM), F32) * 0.1
    hgrn_norm_w = 1.0 + 0.02 * nrm(ks[5], (N_EVEN, B_V_DIM), F32)
    odd_w_in = nrm(ks[6], (N_ODD, D_MODEL, ODD_IN), F32) * D_MODEL ** -0.5
    odd_w_out = nrm(ks[7], (N_ODD, C_DIM, D_MODEL), F32) * C_DIM ** -0.5
    odd_rel_bias = nrm(ks[8], (N_ODD, C_HEADS, 2 * REL_CLIP + 1), F32) * 0.1
    ffn_w_up = nrm(ks[9], (DEPTH, D_MODEL, 2 * D_FF), F32) * D_MODEL ** -0.5
    ffn_conv_w = nrm(ks[10], (DEPTH, CONV_WIDTH, D_FF), F32) * CONV_WIDTH ** -0.5
    ffn_conv_b = nrm(ks[11], (DEPTH, D_FF), F32) * 0.02
    ffn_w_down = nrm(ks[12], (DEPTH, D_FF, D_MODEL), F32) * D_FF ** -0.5
    norm_w = 1.0 + 0.02 * nrm(ks[13], (DEPTH, 4, D_MODEL), F32)
    return {"x": x, "even_w_in": even_w_in, "even_w_out": even_w_out, "even_sinks": even_sinks,
            "hgrn_lb_logits": hgrn_lb_logits, "hgrn_norm_w": hgrn_norm_w,
            "odd_w_in": odd_w_in, "odd_w_out": odd_w_out, "odd_rel_bias": odd_rel_bias,
            "ffn_w_up": ffn_w_up, "ffn_conv_w": ffn_conv_w, "ffn_conv_b": ffn_conv_b,
            "ffn_w_down": ffn_w_down, "norm_w": norm_w}


def reference(x, even_w_in, even_w_out, even_sinks, hgrn_lb_logits, hgrn_norm_w,
              odd_w_in, odd_w_out, odd_rel_bias, ffn_w_up, ffn_conv_w, ffn_conv_b,
              ffn_w_down, norm_w):
    lb_all = jnp.cumsum(jax.nn.softmax(hgrn_lb_logits.astype(F32), axis=0), axis=0)
    h = x
    for layer in range(DEPTH):
        nw = norm_w[layer]
        j = layer // 2
        y = rms_norm(h, nw[0])
        if layer % 2 == 0:
            y = even_mixer(y, even_w_in[j], even_w_out[j], even_sinks[j], lb_all[j], hgrn_norm_w[j])
        else:
            y = odd_mixer(y, odd_w_in[j], odd_w_out[j], odd_rel_bias[j])
        h = h + rms_norm(y, nw[1])
        y = conv_ffn(rms_norm(h, nw[2]), ffn_w_up[layer], ffn_conv_w[layer], ffn_conv_b[layer], ffn_w_down[layer])
        h = h + rms_norm(y, nw[3])
    return h
```

```python
import functools

import numpy as np
import jax
import jax.numpy as jnp
from jax import lax
from jax.experimental import pallas as pl
from jax.experimental.pallas import tpu as pltpu

F32 = jnp.float32
BF16 = jnp.bfloat16

D_MODEL = 1024
CHUNK = 64
HEAD_DIM = 64
ROPE_THETA = 10000.0
EPS = 1e-6
A_Q_HEADS = 8
A_KV_HEADS = 2
A_PREV_CHUNKS = 2
B_KEY_DIM = 128
B_VAL_DIM = 128
B_HEADS = 4
B_BLOCK = 32
C_HEADS = 16
C_PREV_CHUNKS = 8
REL_CLIP = 128
D_FF = 2816
CONV_WIDTH = 3
A_Q_DIM = A_Q_HEADS * HEAD_DIM
A_KV_DIM = A_KV_HEADS * HEAD_DIM
B_QK_DIM = B_HEADS * B_KEY_DIM
B_V_DIM = B_HEADS * B_VAL_DIM
EVEN_IN = A_Q_DIM + 2 * A_KV_DIM + 2 * B_QK_DIM + 2 * B_V_DIM
C_DIM = C_HEADS * HEAD_DIM
ODD_IN = 3 * C_DIM

LANES = 128
VMEM_LIMIT_BYTES = 56 * 1024 * 1024

PROJ_TILE = 512
PROJ_COLS = 256
FFN_COLS = 256
HGRN_TILE = 256
CARRY_ROWS = 8


def _rms(x, w):
    return x * lax.rsqrt(jnp.mean(x * x, axis=-1, keepdims=True) + EPS) * w


def _const_spec(shape):
    nd = len(shape)
    return pl.BlockSpec(shape, lambda *_: (0,) * nd, pipeline_mode=pl.Buffered(1))


def _in_proj_kernel(h_ref, nw_ref, w_ref, cos_ref, sin_ref, o_ref, *, rope_groups):
    y = _rms(h_ref[...], nw_ref[...]).astype(BF16)
    n_out = o_ref.shape[1]
    lane = lax.broadcasted_iota(jnp.int32, (1, LANES), 1)
    first_half = (lane % HEAD_DIM) < (HEAD_DIM // 2)
    for c0 in range(0, n_out, PROJ_COLS):
        acc = jnp.dot(y, w_ref[:, c0:c0 + PROJ_COLS], preferred_element_type=F32)
        for g0 in range(0, PROJ_COLS, LANES):
            part = acc[:, g0:g0 + LANES]
            if (c0 + g0) // LANES < rope_groups:
                partner = jnp.where(first_half, pltpu.roll(part, LANES - HEAD_DIM // 2, 1),
                                    pltpu.roll(part, HEAD_DIM // 2, 1))
                part = part * cos_ref[...] + partner * sin_ref[...]
            o_ref[:, c0 + g0:c0 + g0 + LANES] = part.astype(BF16)


def _in_proj(h, nw, w, cos, sin, rope_groups, seq):
    n, d = h.shape
    n_out = w.shape[1]
    t = min(PROJ_TILE, seq)
    tiles_per_seq = seq // t
    return pl.pallas_call(
        functools.partial(_in_proj_kernel, rope_groups=rope_groups),
        grid=(n // t,),
        in_specs=[
            pl.BlockSpec((t, d), lambda i: (i, 0)),
            _const_spec((1, d)),
            _const_spec((d, n_out)),
            pl.BlockSpec((t, LANES), lambda i: (i % tiles_per_seq, 0)),
            pl.BlockSpec((t, LANES), lambda i: (i % tiles_per_seq, 0)),
        ],
        out_specs=pl.BlockSpec((t, n_out), lambda i: (i, 0)),
        out_shape=jax.ShapeDtypeStruct((n, n_out), BF16),
        compiler_params=pltpu.CompilerParams(
            dimension_semantics=("arbitrary",), vmem_limit_bytes=VMEM_LIMIT_BYTES),
        name="in_proj",
    )(h, nw, w, cos, sin)


def _band_attention_kernel(*refs, n_groups, n_prev, has_bias, has_sinks):
    q_ref, k_ref, v_ref = refs[:3]
    pos = 3
    bias_ref = sink_ref = None
    if has_bias:
        bias_ref = refs[pos]; pos += 1
    if has_sinks:
        sink_ref = refs[pos]; pos += 1
    o_ref, kpad, vpad = refs[pos:pos + 3]

    seq = q_ref.shape[1]
    pad = n_prev * CHUNK
    band = pad + CHUNK
    n_chunks = seq // CHUNK

    kpad[0:pad, :] = jnp.zeros((pad, LANES), BF16)
    vpad[0:pad, :] = jnp.zeros((pad, LANES), BF16)
    kpad[pad:pad + seq, :] = k_ref[0]
    vpad[pad:pad + seq, :] = v_ref[0]

    lane = lax.broadcasted_iota(jnp.int32, (1, LANES), 1)
    low = lane < HEAD_DIM
    key_idx = lax.broadcasted_iota(jnp.int32, (1, band), 1)

    def body(c, carry):
        r0 = pl.multiple_of(c * CHUNK, CHUNK)
        q = q_ref[0, pl.ds(r0, CHUNK), :]
        zero = jnp.zeros((), BF16)
        parts = []
        for j in range(n_groups):
            qj = q[:, j * LANES:(j + 1) * LANES]
            parts.append(jnp.where(low, qj, zero))
            parts.append(jnp.where(low, zero, qj))
        lhs = jnp.concatenate(parts, axis=0)
        kb = kpad[pl.ds(r0, band), :]
        vb = vpad[pl.ds(r0, band), :]
        s = lax.dot_general(lhs, kb, (((1,), (1,)), ((), ())), preferred_element_type=F32)
        if has_bias:
            s = s + bias_ref[0]
        s = jnp.where(key_idx >= pad - r0, s, -jnp.inf)
        m = jnp.max(s, axis=-1, keepdims=True)
        if has_sinks:
            sk = sink_ref[...]
            m = jnp.maximum(m, sk)
        p = jnp.exp(s - m)
        denom = jnp.sum(p, axis=-1, keepdims=True)
        if has_sinks:
            denom = denom + jnp.exp(sk - m)
        o = jnp.dot(p.astype(BF16), vb, preferred_element_type=F32) / denom
        outs = []
        for j in range(n_groups):
            top = o[2 * j * CHUNK:(2 * j + 1) * CHUNK]
            bot = o[(2 * j + 1) * CHUNK:(2 * j + 2) * CHUNK]
            outs.append(jnp.where(low, top, bot))
        o_ref[0, pl.ds(r0, CHUNK), :] = jnp.concatenate(outs, axis=1).astype(BF16)
        return carry

    lax.fori_loop(0, n_chunks, body, 0)


def _band_attention(proj, *, q_col, k_col, v_col, n_groups, groups_per_step, n_prev, bias=None, sinks=None):
    b, seq, _ = proj.shape
    n_steps = n_groups // groups_per_step
    per_head_kv = groups_per_step == 1
    qw = groups_per_step * LANES
    band = (n_prev + 1) * CHUNK
    kv_map = (lambda col: (lambda i, j: (i, 0, col + j))) if per_head_kv else (lambda col: (lambda i, j: (i, 0, col)))
    in_specs = [
        pl.BlockSpec((1, seq, qw), lambda i, j: (i, 0, q_col // groups_per_step + j)),
        pl.BlockSpec((1, seq, LANES), kv_map(k_col)),
        pl.BlockSpec((1, seq, LANES), kv_map(v_col)),
    ]
    args = [proj, proj, proj]
    if bias is not None:
        in_specs.append(pl.BlockSpec((1, 2 * CHUNK * groups_per_step, band), lambda i, j: (j, 0, 0)))
        args.append(bias)
    if sinks is not None:
        in_specs.append(pl.BlockSpec((2 * CHUNK * groups_per_step, 1), lambda i, j: (j, 0)))
        args.append(sinks)
    return pl.pallas_call(
        functools.partial(_band_attention_kernel, n_groups=groups_per_step, n_prev=n_prev,
                          has_bias=bias is not None, has_sinks=sinks is not None),
        grid=(b, n_steps),
        in_specs=in_specs,
        out_specs=pl.BlockSpec((1, seq, qw), lambda i, j: (i, 0, j)),
        out_shape=jax.ShapeDtypeStruct((b, seq, n_groups * LANES), BF16),
        scratch_shapes=[pltpu.VMEM((seq + n_prev * CHUNK, LANES), BF16),
                        pltpu.VMEM((seq + n_prev * CHUNK, LANES), BF16)],
        compiler_params=pltpu.CompilerParams(
            dimension_semantics=("arbitrary", "arbitrary"), vmem_limit_bytes=VMEM_LIMIT_BYTES),
        name="band_attention",
    )(*args)


def _hgrn_kernel(q_ref, f_ref, i_ref, g_ref, lbl_ref, nw_ref, o_ref, *, layer_j):
    seq = q_ref.shape[1]
    t = min(HGRN_TILE, seq)
    n_tiles = seq // t
    n_blk = t // B_BLOCK

    logits = lbl_ref[...]
    e = jnp.exp(logits - jnp.max(logits, axis=0, keepdims=True))
    sm = e / jnp.sum(e, axis=0, keepdims=True)
    lb = sm[0:1]
    for r in range(1, layer_j + 1):
        lb = lb + sm[r:r + 1]

    row = lax.broadcasted_iota(jnp.int32, (t, t), 0)
    col = lax.broadcasted_iota(jnp.int32, (t, t), 1)
    same_blk = (row // B_BLOCK) == (col // B_BLOCK)
    tril = jnp.logical_and(same_blk, col <= row)
    cum_mat = jnp.where(tril, 1.0, 0.0).astype(F32)
    tot_mat = jnp.where(same_blk, 1.0, 0.0).astype(F32)
    nw = nw_ref[...]

    def tile_body(ti, state_t):
        r0 = pl.multiple_of(ti * t, t)
        fl = f_ref[0, pl.ds(r0, t), :].astype(F32)
        f = lb + (1.0 - lb) * jax.nn.sigmoid(fl)
        log_f = jnp.log(f)
        cum = jnp.dot(cum_mat, log_f, precision=lax.Precision.HIGHEST, preferred_element_type=F32)
        last = jnp.dot(tot_mat, log_f, precision=lax.Precision.HIGHEST, preferred_element_type=F32)
        k = 1.0 - f
        q = q_ref[0, pl.ds(r0, t), :].astype(F32)
        q_dec = (q * jnp.exp(cum)).astype(BF16)
        k_dec = (k * jnp.exp(-cum)).astype(BF16)
        k_end = (k * jnp.exp(last - cum)).astype(BF16)
        blk_decay = jnp.exp(last)
        v = i_ref[0, pl.ds(r0, t), :]
        s = lax.dot_general(q_dec, k_dec, (((1,), (1,)), ((), ())), preferred_element_type=F32)
        s = jnp.where(tril, s, 0.0)
        o_intra = jnp.dot(s.astype(BF16), v, preferred_element_type=F32)
        outs = []
        for n in range(n_blk):
            sl = slice(n * B_BLOCK, (n + 1) * B_BLOCK)
            outs.append(lax.dot_general(q_dec[sl], state_t.astype(BF16), (((1,), (1,)), ((), ())),
                                        preferred_element_type=F32))
            upd = lax.dot_general(v[sl], k_end[sl], (((0,), (0,)), ((), ())), preferred_element_type=F32)
            state_t = blk_decay[n * B_BLOCK:n * B_BLOCK + 1] * state_t + upd
        o = o_intra + jnp.concatenate(outs, axis=0)
        o = _rms(o, nw)
        g = g_ref[0, pl.ds(r0, t), :].astype(F32)
        o_ref[0, pl.ds(r0, t), :] = (o * (g * jax.nn.sigmoid(g))).astype(BF16)
        return state_t

    lax.fori_loop(0, n_tiles, tile_body, jnp.zeros((B_VAL_DIM, B_KEY_DIM), F32))


def _hgrn(proj, lb_logits, norm_w, *, q_col, f_col, i_col, g_col, layer_j):
    b, seq, _ = proj.shape
    n_layers = lb_logits.shape[0]

    def col_spec(col):
        return pl.BlockSpec((1, seq, LANES), lambda i, h: (i, 0, col + h))

    return pl.pallas_call(
        functools.partial(_hgrn_kernel, layer_j=layer_j),
        grid=(b, B_HEADS),
        in_specs=[col_spec(q_col), col_spec(f_col), col_spec(i_col), col_spec(g_col),
                  pl.BlockSpec((n_layers, LANES), lambda i, h: (0, h)),
                  pl.BlockSpec((1, LANES), lambda i, h: (0, h))],
        out_specs=pl.BlockSpec((1, seq, LANES), lambda i, h: (i, 0, h)),
        out_shape=jax.ShapeDtypeStruct((b, seq, B_V_DIM), BF16),
        compiler_params=pltpu.CompilerParams(
            dimension_semantics=("arbitrary", "arbitrary"), vmem_limit_bytes=VMEM_LIMIT_BYTES),
        name="hgrn2",
    )(proj, proj, proj, proj, lb_logits, norm_w)


def _out_ffn_kernel(*refs, n_mix, tiles_per_seq):
    mix_refs = refs[:n_mix]
    (h_ref, wout_ref, nw_ref, wup_ref, cw_ref, cb_ref, wdn_ref,
     hout_ref, g_scr, ubuf, carry) = refs[n_mix:]
    t = h_ref.shape[0]

    @pl.when(pl.program_id(0) % tiles_per_seq == 0)
    def _():
        carry[...] = jnp.zeros(carry.shape, F32)

    k0 = 0
    mixed = None
    for r in mix_refs:
        kw = r.shape[1]
        part = jnp.dot(r[...], wout_ref[k0:k0 + kw, :], preferred_element_type=F32)
        mixed = part if mixed is None else mixed + part
        k0 += kw
    h_mid = h_ref[...] + _rms(mixed, nw_ref[1:2])
    y = _rms(h_mid, nw_ref[2:3]).astype(BF16)

    sqrt_half = np.float32(np.sqrt(0.5))
    for c0 in range(0, D_FF, FFN_COLS):
        u = jnp.dot(y, wup_ref[:, c0:c0 + FFN_COLS], preferred_element_type=F32)
        v = jnp.dot(y, wup_ref[:, D_FF + c0:D_FF + c0 + FFN_COLS], preferred_element_type=F32)
        ubuf[0:CARRY_ROWS, :] = carry[:, c0:c0 + FFN_COLS]
        ubuf[CARRY_ROWS:CARRY_ROWS + t, :] = u
        carry[:, c0:c0 + FFN_COLS] = u[t - CARRY_ROWS:t]
        u1 = ubuf[CARRY_ROWS - 1:CARRY_ROWS - 1 + t, :]
        u2 = ubuf[CARRY_ROWS - 2:CARRY_ROWS - 2 + t, :]
        c = cb_ref[:, c0:c0 + FFN_COLS] + cw_ref[0:1, c0:c0 + FFN_COLS] * u2
        c = c + cw_ref[1:2, c0:c0 + FFN_COLS] * u1
        c = c + cw_ref[2:3, c0:c0 + FFN_COLS] * u
        gelu = 0.5 * c * (1.0 + lax.erf(c * sqrt_half))
        g_scr[:, c0:c0 + FFN_COLS] = (gelu * v).astype(BF16)

    down = jnp.dot(g_scr[...], wdn_ref[...], preferred_element_type=F32)
    hout_ref[...] = h_mid + _rms(down, nw_ref[3:4])


def _out_ffn(mix_list, h, w_out, nw, w_up, conv_w, conv_b, w_down, seq):
    n, d = h.shape
    t = min(PROJ_TILE, seq)
    tiles_per_seq = seq // t
    in_specs = [pl.BlockSpec((t, m.shape[1]), lambda i: (i, 0)) for m in mix_list]
    in_specs += [
        pl.BlockSpec((t, d), lambda i: (i, 0)),
        _const_spec(w_out.shape),
        _const_spec(nw.shape),
        _const_spec(w_up.shape),
        _const_spec(conv_w.shape),
        _const_spec(conv_b.shape),
        _const_spec(w_down.shape),
    ]
    return pl.pallas_call(
        functools.partial(_out_ffn_kernel, n_mix=len(mix_list), tiles_per_seq=tiles_per_seq),
        grid=(n // t,),
        in_specs=in_specs,
        out_specs=pl.BlockSpec((t, d), lambda i: (i, 0)),
        out_shape=jax.ShapeDtypeStruct((n, d), F32),
        scratch_shapes=[pltpu.VMEM((t, D_FF), BF16),
                        pltpu.VMEM((t + CARRY_ROWS, FFN_COLS), F32),
                        pltpu.VMEM((CARRY_ROWS, D_FF), F32)],
        compiler_params=pltpu.CompilerParams(
            dimension_semantics=("arbitrary",), vmem_limit_bytes=VMEM_LIMIT_BYTES),
        name="out_ffn",
    )(*mix_list, h, w_out, nw, w_up, conv_w, conv_b, w_down)


def _rope_tables(seq):
    inv_freq = ROPE_THETA ** (-jnp.arange(0, HEAD_DIM, 2, dtype=F32) / HEAD_DIM)
    ang = jnp.arange(seq, dtype=F32)[:, None] * inv_freq[None, :]
    cos, sin = jnp.cos(ang), jnp.sin(ang)
    reps = LANES // HEAD_DIM
    cos_t = jnp.tile(jnp.concatenate([cos, cos], axis=-1), (1, reps))
    sin_t = jnp.tile(jnp.concatenate([-sin, sin], axis=-1), (1, reps))
    return cos_t, sin_t


_A_HEAD_ORDER = tuple(h for j in range(A_Q_HEADS // 2) for h in (j, j + A_Q_HEADS // 2))


def _head_cols(order):
    return np.concatenate([np.arange(h * HEAD_DIM, (h + 1) * HEAD_DIM) for h in order])


def kernel(x, even_w_in, even_w_out, even_sinks, hgrn_lb_logits, hgrn_norm_w, odd_w_in, odd_w_out,
           odd_rel_bias, ffn_w_up, ffn_conv_w, ffn_conv_b, ffn_w_down, norm_w):
    b, seq, d = x.shape
    depth = norm_w.shape[0]
    scale = HEAD_DIM ** -0.5
    cos_t, sin_t = _rope_tables(seq)
    a_cols = _head_cols(_A_HEAD_ORDER)

    h = x.reshape(b * seq, d)
    for layer in range(depth):
        nw = norm_w[layer]
        j = layer // 2
        if layer % 2 == 0:
            w_in = even_w_in[j]
            w_in = jnp.concatenate([w_in[:, :A_Q_DIM][:, a_cols] * scale, w_in[:, A_Q_DIM:]], axis=1).astype(BF16)
            w_out = jnp.concatenate([even_w_out[j][:A_Q_DIM][a_cols], even_w_out[j][A_Q_DIM:]], axis=0).astype(BF16)
            rope_groups = (A_Q_DIM + A_KV_DIM) // LANES
            proj = _in_proj(h, nw[0:1], w_in, cos_t, sin_t, rope_groups, seq).reshape(b, seq, EVEN_IN)
            sinks = jnp.repeat(even_sinks[j][np.array(_A_HEAD_ORDER)], CHUNK)[:, None].astype(F32)
            n_groups = A_Q_DIM // LANES
            oa = _band_attention(proj, q_col=0, k_col=n_groups, v_col=n_groups + 1, n_groups=n_groups,
                                 groups_per_step=n_groups, n_prev=A_PREV_CHUNKS, sinks=sinks)
            c0 = (A_Q_DIM + 2 * A_KV_DIM) // LANES
            ob = _hgrn(proj, hgrn_lb_logits, hgrn_norm_w[j:j + 1],
                       q_col=c0, f_col=c0 + B_HEADS, i_col=c0 + 2 * B_HEADS, g_col=c0 + 3 * B_HEADS, layer_j=j)
            mix = [oa.reshape(b * seq, A_Q_DIM), ob.reshape(b * seq, B_V_DIM)]
        else:
            w_in = odd_w_in[j]
            w_in = jnp.concatenate([w_in[:, :C_DIM] * scale, w_in[:, C_DIM:]], axis=1).astype(BF16)
            w_out = odd_w_out[j].astype(BF16)
            proj = _in_proj(h, nw[0:1], w_in, cos_t, sin_t, 0, seq).reshape(b, seq, ODD_IN)
            pad = C_PREV_CHUNKS * CHUNK
            band = pad + CHUNK
            rel = jnp.arange(CHUNK)[:, None] + pad - jnp.arange(band)[None, :]
            idx = jnp.clip(rel, -REL_CLIP, REL_CLIP) + REL_CLIP
            n_groups = C_DIM // LANES
            bias = odd_rel_bias[j][:, idx].astype(F32).reshape(n_groups, 2 * CHUNK, band)
            oc = _band_attention(proj, q_col=0, k_col=n_groups, v_col=2 * n_groups, n_groups=n_groups,
                                 groups_per_step=1, n_prev=C_PREV_CHUNKS, bias=bias)
            mix = [oc.reshape(b * seq, C_DIM)]
        h = _out_ffn(mix, h, w_out, nw, ffn_w_up[layer].astype(BF16), ffn_conv_w[layer],
                     ffn_conv_b[layer][None, :], ffn_w_down[layer].astype(BF16), seq)
    return h.reshape(b, seq, d)
```

```python
import functools

import numpy as np
import jax
import jax.numpy as jnp
from jax import lax
from jax.experimental import pallas as pl
from jax.experimental.pallas import tpu as pltpu

F32 = jnp.float32
BF16 = jnp.bfloat16

D_MODEL = 1024
CHUNK = 64
HEAD_DIM = 64
ROPE_THETA = 10000.0
EPS = 1e-6
A_Q_HEADS = 8
A_KV_HEADS = 2
A_PREV_CHUNKS = 2
B_KEY_DIM = 128
B_VAL_DIM = 128
B_HEADS = 4
B_BLOCK = 32
C_HEADS = 16
C_PREV_CHUNKS = 8
REL_CLIP = 128
D_FF = 2816
CONV_WIDTH = 3
A_Q_DIM = A_Q_HEADS * HEAD_DIM
A_KV_DIM = A_KV_HEADS * HEAD_DIM
B_QK_DIM = B_HEADS * B_KEY_DIM
B_V_DIM = B_HEADS * B_VAL_DIM
EVEN_IN = A_Q_DIM + 2 * A_KV_DIM + 2 * B_QK_DIM + 2 * B_V_DIM
C_DIM = C_HEADS * HEAD_DIM
ODD_IN = 3 * C_DIM

LANES = 128
VMEM_LIMIT_BYTES = 56 * 1024 * 1024

PROJ_TILE = 512
PROJ_COLS = 256
FFN_COLS = 256
HGRN_TILE = 256
HGRN_UNROLL = 4
ATTN_A_TQ = 128
ATTN_C_TQ = 256
ATTN_UNROLL = 8
CARRY_ROWS = 8


def _rms(x, w):
    return x * lax.rsqrt(jnp.mean(x * x, axis=-1, keepdims=True) + EPS) * w


def _const_spec(shape):
    nd = len(shape)
    return pl.BlockSpec(shape, lambda *_: (0,) * nd, pipeline_mode=pl.Buffered(1))


def _in_proj_kernel(h_ref, nw_ref, w_ref, cos_ref, sin_ref, o_ref, *, rope_groups):
    y = _rms(h_ref[...], nw_ref[...]).astype(BF16)
    n_out = o_ref.shape[1]
    lane = lax.broadcasted_iota(jnp.int32, (1, LANES), 1)
    first_half = (lane % HEAD_DIM) < (HEAD_DIM // 2)
    for c0 in range(0, n_out, PROJ_COLS):
        acc = jnp.dot(y, w_ref[:, c0:c0 + PROJ_COLS], preferred_element_type=F32)
        for g0 in range(0, PROJ_COLS, LANES):
            part = acc[:, g0:g0 + LANES]
            if (c0 + g0) // LANES < rope_groups:
                partner = jnp.where(first_half, pltpu.roll(part, LANES - HEAD_DIM // 2, 1),
                                    pltpu.roll(part, HEAD_DIM // 2, 1))
                part = part * cos_ref[...] + partner * sin_ref[...]
            o_ref[:, c0 + g0:c0 + g0 + LANES] = part.astype(BF16)


def _in_proj(h, nw, w, cos, sin, rope_groups, seq):
    n, d = h.shape
    n_out = w.shape[1]
    t = min(PROJ_TILE, seq)
    tiles_per_seq = seq // t
    return pl.pallas_call(
        functools.partial(_in_proj_kernel, rope_groups=rope_groups),
        grid=(n // t,),
        in_specs=[
            pl.BlockSpec((t, d), lambda i: (i, 0)),
            _const_spec((1, d)),
            _const_spec((d, n_out)),
            pl.BlockSpec((t, LANES), lambda i: (i % tiles_per_seq, 0)),
            pl.BlockSpec((t, LANES), lambda i: (i % tiles_per_seq, 0)),
        ],
        out_specs=pl.BlockSpec((t, n_out), lambda i: (i, 0)),
        out_shape=jax.ShapeDtypeStruct((n, n_out), BF16),
        compiler_params=pltpu.CompilerParams(
            dimension_semantics=("arbitrary",), vmem_limit_bytes=VMEM_LIMIT_BYTES),
        name="in_proj",
    )(h, nw, w, cos, sin)


def _band_attention_kernel(*refs, n_groups, n_prev, tq, has_sinks, unroll):
    q_ref, k_ref, v_ref, bias_ref = refs[:4]
    sink_ref = refs[4] if has_sinks else None
    o_ref, kpad, vpad = refs[4 + has_sinks:]

    seq = q_ref.shape[1]
    pad = n_prev * CHUNK
    tk = pad + tq
    n_tiles = seq // tq

    kpad[0:pad, :] = jnp.zeros((pad, LANES), BF16)
    vpad[0:pad, :] = jnp.zeros((pad, LANES), BF16)
    kpad[pad:pad + seq, :] = k_ref[0]
    vpad[pad:pad + seq, :] = v_ref[0]

    lane = lax.broadcasted_iota(jnp.int32, (1, LANES), 1)
    low = lane < HEAD_DIM
    key_idx = lax.broadcasted_iota(jnp.int32, (1, tk), 1)
    zero = jnp.zeros((), BF16)

    def body(ti, carry):
        r0 = pl.multiple_of(ti * tq, tq)
        q = q_ref[0, pl.ds(r0, tq), :]
        parts = []
        for j in range(n_groups):
            qj = q[:, j * LANES:(j + 1) * LANES]
            parts.append(jnp.where(low, qj, zero))
            parts.append(jnp.where(low, zero, qj))
        lhs = jnp.concatenate(parts, axis=0)
        kb = kpad[pl.ds(r0, tk), :]
        vb = vpad[pl.ds(r0, tk), :]
        s = lax.dot_general(lhs, kb, (((1,), (1,)), ((), ())), preferred_element_type=F32)
        s = s + bias_ref[0]
        s = jnp.where(key_idx >= pad - r0, s, -jnp.inf)
        m = jnp.max(s, axis=-1, keepdims=True)
        if has_sinks:
            sk = sink_ref[...]
            m = jnp.maximum(m, sk)
        p = jnp.exp(s - m)
        denom = jnp.sum(p, axis=-1, keepdims=True)
        if has_sinks:
            denom = denom + jnp.exp(sk - m)
        o = jnp.dot(p.astype(BF16), vb, preferred_element_type=F32) / denom
        outs = []
        for j in range(n_groups):
            top = o[2 * j * tq:(2 * j + 1) * tq]
            bot = o[(2 * j + 1) * tq:(2 * j + 2) * tq]
            outs.append(jnp.where(low, top, bot))
        o_ref[0, pl.ds(r0, tq), :] = jnp.concatenate(outs, axis=1).astype(BF16)
        return carry

    lax.fori_loop(0, n_tiles, body, 0, unroll=unroll)


def _band_mask(tq, n_prev):
    qc = np.arange(tq)[:, None] // CHUNK
    kc = np.arange(tq + n_prev * CHUNK)[None, :] // CHUNK
    return np.where((kc >= qc) & (kc <= qc + n_prev), 0.0, -np.inf).astype(np.float32)


def _band_attention(proj, bias, *, q_col, k_col, v_col, n_groups, groups_per_step, n_prev, tq, unroll, sinks=None):
    b, seq, _ = proj.shape
    n_steps = n_groups // groups_per_step
    per_head_kv = groups_per_step == 1
    qw = groups_per_step * LANES
    tk = tq + n_prev * CHUNK
    rows = groups_per_step * 2 * tq
    kv_map = (lambda col: (lambda j, i: (i, 0, col + j))) if per_head_kv else (lambda col: (lambda j, i: (i, 0, col)))
    in_specs = [
        pl.BlockSpec((1, seq, qw), lambda j, i: (i, 0, q_col // groups_per_step + j)),
        pl.BlockSpec((1, seq, LANES), kv_map(k_col)),
        pl.BlockSpec((1, seq, LANES), kv_map(v_col)),
        pl.BlockSpec((1, rows, tk), lambda j, i: (j, 0, 0)),
    ]
    args = [proj, proj, proj, bias]
    if sinks is not None:
        in_specs.append(pl.BlockSpec((rows, 1), lambda j, i: (j, 0)))
        args.append(sinks)
    return pl.pallas_call(
        functools.partial(_band_attention_kernel, n_groups=groups_per_step, n_prev=n_prev, tq=tq,
                          has_sinks=sinks is not None, unroll=unroll),
        grid=(n_steps, b),
        in_specs=in_specs,
        out_specs=pl.BlockSpec((1, seq, qw), lambda j, i: (i, 0, j)),
        out_shape=jax.ShapeDtypeStruct((b, seq, n_groups * LANES), BF16),
        scratch_shapes=[pltpu.VMEM((seq + n_prev * CHUNK, LANES), BF16),
                        pltpu.VMEM((seq + n_prev * CHUNK, LANES), BF16)],
        compiler_params=pltpu.CompilerParams(
            dimension_semantics=("arbitrary", "arbitrary"), vmem_limit_bytes=VMEM_LIMIT_BYTES),
        name="band_attention",
    )(*args)


def _hgrn_kernel(q_ref, f_ref, i_ref, g_ref, lbl_ref, nw_ref, o_ref, *, layer_j):
    seq = q_ref.shape[1]
    t = min(HGRN_TILE, seq)
    n_tiles = seq // t
    n_blk = t // B_BLOCK

    logits = lbl_ref[...]
    e = jnp.exp(logits - jnp.max(logits, axis=0, keepdims=True))
    sm = e / jnp.sum(e, axis=0, keepdims=True)
    lb = sm[0:1]
    for r in range(1, layer_j + 1):
        lb = lb + sm[r:r + 1]

    row = lax.broadcasted_iota(jnp.int32, (t, t), 0)
    col = lax.broadcasted_iota(jnp.int32, (t, t), 1)
    same_blk = (row // B_BLOCK) == (col // B_BLOCK)
    tril = jnp.logical_and(same_blk, col <= row)
    cum_mat = jnp.where(tril, 1.0, 0.0).astype(BF16)
    nw = nw_ref[...]

    def block_diag(x):
        cols = []
        for n in range(n_blk):
            pieces = [x[n * B_BLOCK:(n + 1) * B_BLOCK]]
            if n > 0:
                pieces.insert(0, jnp.zeros((n * B_BLOCK, x.shape[1]), x.dtype))
            if n < n_blk - 1:
                pieces.append(jnp.zeros(((n_blk - 1 - n) * B_BLOCK, x.shape[1]), x.dtype))
            cols.append(jnp.concatenate(pieces, axis=0))
        return jnp.concatenate(cols, axis=1)

    def tile_body(ti, state_t):
        r0 = pl.multiple_of(ti * t, t)
        fl = f_ref[0, pl.ds(r0, t), :].astype(F32)
        f = lb + (1.0 - lb) * jax.nn.sigmoid(fl)
        log_f = jnp.log(f)
        hi = log_f.astype(BF16)
        r1 = log_f - hi.astype(F32)
        mid = r1.astype(BF16)
        lo = (r1 - mid.astype(F32)).astype(BF16)
        parts = jnp.dot(cum_mat, jnp.concatenate([hi, mid, lo], axis=1), preferred_element_type=F32)
        cum = (parts[:, 0:LANES] + parts[:, LANES:2 * LANES]) + parts[:, 2 * LANES:3 * LANES]
        last = jnp.broadcast_to(cum.reshape(n_blk, B_BLOCK, B_KEY_DIM)[:, B_BLOCK - 1:B_BLOCK, :],
                                (n_blk, B_BLOCK, B_KEY_DIM)).reshape(t, B_KEY_DIM)
        k = 1.0 - f
        q = q_ref[0, pl.ds(r0, t), :].astype(F32)
        q_dec = (q * jnp.exp(cum)).astype(BF16)
        k_dec = (k * jnp.exp(-cum)).astype(BF16)
        k_end = (k * jnp.exp(last - cum)).astype(BF16)
        blk_decay = jnp.exp(last)
        v = i_ref[0, pl.ds(r0, t), :]
        s = lax.dot_general(q_dec, k_dec, (((1,), (1,)), ((), ())), preferred_element_type=F32)
        s = jnp.where(tril, s, 0.0)
        o_intra = jnp.dot(s.astype(BF16), v, preferred_element_type=F32)
        upd = lax.dot_general(v, block_diag(k_end), (((0,), (0,)), ((), ())), preferred_element_type=F32)
        prev_states = []
        for n in range(n_blk):
            prev_states.append(state_t.astype(BF16))
            state_t = (blk_decay[n * B_BLOCK:n * B_BLOCK + 1] * state_t
                       + upd[:, n * B_KEY_DIM:(n + 1) * B_KEY_DIM])
        o_inter = lax.dot_general(block_diag(q_dec), jnp.concatenate(prev_states, axis=1),
                                  (((1,), (1,)), ((), ())), preferred_element_type=F32)
        o = o_intra + o_inter
        o = _rms(o, nw)
        g = g_ref[0, pl.ds(r0, t), :].astype(F32)
        o_ref[0, pl.ds(r0, t), :] = (o * (g * jax.nn.sigmoid(g))).astype(BF16)
        return state_t

    lax.fori_loop(0, n_tiles, tile_body, jnp.zeros((B_VAL_DIM, B_KEY_DIM), F32), unroll=HGRN_UNROLL)


def _hgrn(proj, lb_logits, norm_w, *, q_col, f_col, i_col, g_col, layer_j):
    b, seq, _ = proj.shape
    n_layers = lb_logits.shape[0]

    def col_spec(col):
        return pl.BlockSpec((1, seq, LANES), lambda i, h: (i, 0, col + h))

    return pl.pallas_call(
        functools.partial(_hgrn_kernel, layer_j=layer_j),
        grid=(b, B_HEADS),
        in_specs=[col_spec(q_col), col_spec(f_col), col_spec(i_col), col_spec(g_col),
                  pl.BlockSpec((n_layers, LANES), lambda i, h: (0, h)),
                  pl.BlockSpec((1, LANES), lambda i, h: (0, h))],
        out_specs=pl.BlockSpec((1, seq, LANES), lambda i, h: (i, 0, h)),
        out_shape=jax.ShapeDtypeStruct((b, seq, B_V_DIM), BF16),
        compiler_params=pltpu.CompilerParams(
            dimension_semantics=("arbitrary", "arbitrary"), vmem_limit_bytes=VMEM_LIMIT_BYTES),
        name="hgrn2",
    )(proj, proj, proj, proj, lb_logits, norm_w)


def _out_ffn_kernel(*refs, n_mix, tiles_per_seq):
    mix_refs = refs[:n_mix]
    (h_ref, wout_ref, nw_ref, wup_ref, cw_ref, cb_ref, wdn_ref,
     hout_ref, g_scr, ubuf, carry) = refs[n_mix:]
    t = h_ref.shape[0]

    @pl.when(pl.program_id(0) % tiles_per_seq == 0)
    def _():
        carry[...] = jnp.zeros(carry.shape, F32)

    k0 = 0
    mixed = None
    for r in mix_refs:
        kw = r.shape[1]
        part = jnp.dot(r[...], wout_ref[k0:k0 + kw, :], preferred_element_type=F32)
        mixed = part if mixed is None else mixed + part
        k0 += kw
    h_mid = h_ref[...] + _rms(mixed, nw_ref[1:2])
    y = _rms(h_mid, nw_ref[2:3]).astype(BF16)

    sqrt_half = np.float32(np.sqrt(0.5))
    for c0 in range(0, D_FF, FFN_COLS):
        u = jnp.dot(y, wup_ref[:, c0:c0 + FFN_COLS], preferred_element_type=F32)
        v = jnp.dot(y, wup_ref[:, D_FF + c0:D_FF + c0 + FFN_COLS], preferred_element_type=F32)
        ubuf[0:CARRY_ROWS, :] = carry[:, c0:c0 + FFN_COLS]
        ubuf[CARRY_ROWS:CARRY_ROWS + t, :] = u
        carry[:, c0:c0 + FFN_COLS] = u[t - CARRY_ROWS:t]
        u1 = ubuf[CARRY_ROWS - 1:CARRY_ROWS - 1 + t, :]
        u2 = ubuf[CARRY_ROWS - 2:CARRY_ROWS - 2 + t, :]
        c = cb_ref[:, c0:c0 + FFN_COLS] + cw_ref[0:1, c0:c0 + FFN_COLS] * u2
        c = c + cw_ref[1:2, c0:c0 + FFN_COLS] * u1
        c = c + cw_ref[2:3, c0:c0 + FFN_COLS] * u
        gelu = 0.5 * c * (1.0 + lax.erf(c * sqrt_half))
        g_scr[:, c0:c0 + FFN_COLS] = (gelu * v).astype(BF16)

    down = jnp.dot(g_scr[...], wdn_ref[...], preferred_element_type=F32)
    hout_ref[...] = h_mid + _rms(down, nw_ref[3:4])


def _out_ffn(mix_list, h, w_out, nw, w_up, conv_w, conv_b, w_down, seq):
    n, d = h.shape
    t = min(PROJ_TILE, seq)
    tiles_per_seq = seq // t
    in_specs = [pl.BlockSpec((t, m.shape[1]), lambda i: (i, 0)) for m in mix_list]
    in_specs += [
        pl.BlockSpec((t, d), lambda i: (i, 0)),
        _const_spec(w_out.shape),
        _const_spec(nw.shape),
        _const_spec(w_up.shape),
        _const_spec(conv_w.shape),
        _const_spec(conv_b.shape),
        _const_spec(w_down.shape),
    ]
    return pl.pallas_call(
        functools.partial(_out_ffn_kernel, n_mix=len(mix_list), tiles_per_seq=tiles_per_seq),
        grid=(n // t,),
        in_specs=in_specs,
        out_specs=pl.BlockSpec((t, d), lambda i: (i, 0)),
        out_shape=jax.ShapeDtypeStruct((n, d), F32),
        scratch_shapes=[pltpu.VMEM((t, D_FF), BF16),
                        pltpu.VMEM((t + CARRY_ROWS, FFN_COLS), F32),
                        pltpu.VMEM((CARRY_ROWS, D_FF), F32)],
        compiler_params=pltpu.CompilerParams(
            dimension_semantics=("arbitrary",), vmem_limit_bytes=VMEM_LIMIT_BYTES),
        name="out_ffn",
    )(*mix_list, h, w_out, nw, w_up, conv_w, conv_b, w_down)


def _rope_tables(seq):
    inv_freq = ROPE_THETA ** (-jnp.arange(0, HEAD_DIM, 2, dtype=F32) / HEAD_DIM)
    ang = jnp.arange(seq, dtype=F32)[:, None] * inv_freq[None, :]
    cos, sin = jnp.cos(ang), jnp.sin(ang)
    reps = LANES // HEAD_DIM
    cos_t = jnp.tile(jnp.concatenate([cos, cos], axis=-1), (1, reps))
    sin_t = jnp.tile(jnp.concatenate([-sin, sin], axis=-1), (1, reps))
    return cos_t, sin_t


_A_HEAD_ORDER = tuple(h for j in range(A_Q_HEADS // 2) for h in (j, j + A_Q_HEADS // 2))


def _head_cols(order):
    return np.concatenate([np.arange(h * HEAD_DIM, (h + 1) * HEAD_DIM) for h in order])


def kernel(x, even_w_in, even_w_out, even_sinks, hgrn_lb_logits, hgrn_norm_w, odd_w_in, odd_w_out,
           odd_rel_bias, ffn_w_up, ffn_conv_w, ffn_conv_b, ffn_w_down, norm_w):
    b, seq, d = x.shape
    depth = norm_w.shape[0]
    scale = HEAD_DIM ** -0.5
    cos_t, sin_t = _rope_tables(seq)
    a_cols = _head_cols(_A_HEAD_ORDER)

    h = x.reshape(b * seq, d)
    for layer in range(depth):
        nw = norm_w[layer]
        j = layer // 2
        if layer % 2 == 0:
            w_in = even_w_in[j]
            w_in = jnp.concatenate([w_in[:, :A_Q_DIM][:, a_cols] * scale, w_in[:, A_Q_DIM:]], axis=1).astype(BF16)
            w_out = jnp.concatenate([even_w_out[j][:A_Q_DIM][a_cols], even_w_out[j][A_Q_DIM:]], axis=0).astype(BF16)
            rope_groups = (A_Q_DIM + A_KV_DIM) // LANES
            proj = _in_proj(h, nw[0:1], w_in, cos_t, sin_t, rope_groups, seq).reshape(b, seq, EVEN_IN)
            tq = min(ATTN_A_TQ, seq)
            sinks = jnp.repeat(even_sinks[j][np.array(_A_HEAD_ORDER)], tq)[:, None].astype(F32)
            n_groups = A_Q_DIM // LANES
            mask = jnp.asarray(np.tile(_band_mask(tq, A_PREV_CHUNKS), (2 * n_groups, 1))[None])
            oa = _band_attention(proj, mask, q_col=0, k_col=n_groups, v_col=n_groups + 1, n_groups=n_groups,
                                 groups_per_step=n_groups, n_prev=A_PREV_CHUNKS, tq=tq, unroll=ATTN_UNROLL, sinks=sinks)
            c0 = (A_Q_DIM + 2 * A_KV_DIM) // LANES
            ob = _hgrn(proj, hgrn_lb_logits, hgrn_norm_w[j:j + 1],
                       q_col=c0, f_col=c0 + B_HEADS, i_col=c0 + 2 * B_HEADS, g_col=c0 + 3 * B_HEADS, layer_j=j)
            mix = [oa.reshape(b * seq, A_Q_DIM), ob.reshape(b * seq, B_V_DIM)]
        else:
            w_in = odd_w_in[j]
            w_in = jnp.concatenate([w_in[:, :C_DIM] * scale, w_in[:, C_DIM:]], axis=1).astype(BF16)
            w_out = odd_w_out[j].astype(BF16)
            proj = _in_proj(h, nw[0:1], w_in, cos_t, sin_t, 0, seq).reshape(b, seq, ODD_IN)
            pad = C_PREV_CHUNKS * CHUNK
            band = pad + CHUNK
            rel = jnp.arange(CHUNK)[:, None] + pad - jnp.arange(band)[None, :]
            idx = jnp.clip(rel, -REL_CLIP, REL_CLIP) + REL_CLIP
            n_groups = C_DIM // LANES
            tq = min(ATTN_C_TQ, seq)
            n_qc = tq // CHUNK
            rel_bias = odd_rel_bias[j][:, idx].astype(F32)
            bias = jnp.stack([jnp.pad(rel_bias, ((0, 0), (0, 0), (qc * CHUNK, (n_qc - 1 - qc) * CHUNK)),
                                      constant_values=-jnp.inf) for qc in range(n_qc)], axis=1)
            bias = bias.reshape(n_groups, 2 * tq, tq + pad)
            oc = _band_attention(proj, bias, q_col=0, k_col=n_groups, v_col=2 * n_groups, n_groups=n_groups,
                                 groups_per_step=1, n_prev=C_PREV_CHUNKS, tq=tq, unroll=ATTN_UNROLL)
            mix = [oc.reshape(b * seq, C_DIM)]
        h = _out_ffn(mix, h, w_out, nw, ffn_w_up[layer].astype(BF16), ffn_conv_w[layer],
                     ffn_conv_b[layer][None, :], ffn_w_down[layer].astype(BF16), seq)
    return h.reshape(b, seq, d)
```

```python
import functools

import numpy as np
import jax
import jax.numpy as jnp
from jax import lax
from jax.experimental import pallas as pl
from jax.experimental.pallas import tpu as pltpu

F32 = jnp.float32
BF16 = jnp.bfloat16

D_MODEL = 1024
CHUNK = 64
HEAD_DIM = 64
ROPE_THETA = 10000.0
EPS = 1e-6
A_Q_HEADS = 8
A_KV_HEADS = 2
A_PREV_CHUNKS = 2
B_KEY_DIM = 128
B_VAL_DIM = 128
B_HEADS = 4
B_BLOCK = 32
C_HEADS = 16
C_PREV_CHUNKS = 8
REL_CLIP = 128
D_FF = 2816
CONV_WIDTH = 3
A_Q_DIM = A_Q_HEADS * HEAD_DIM
A_KV_DIM = A_KV_HEADS * HEAD_DIM
B_QK_DIM = B_HEADS * B_KEY_DIM
B_V_DIM = B_HEADS * B_VAL_DIM
EVEN_IN = A_Q_DIM + 2 * A_KV_DIM + 2 * B_QK_DIM + 2 * B_V_DIM
C_DIM = C_HEADS * HEAD_DIM
ODD_IN = 3 * C_DIM

LANES = 128
VMEM_LIMIT_BYTES = 56 * 1024 * 1024

PROJ_TILE = 512
PROJ_COLS = 256
FFN_COLS = 256
HGRN_TILE = 256
HGRN_UNROLL = 4
ATTN_A_TQ = 128
ATTN_C_TQ = 256
ONES_ROWS = 16
LOG2E = float(np.log2(np.e))
CARRY_ROWS = 8


def _rms(x, w):
    return x * lax.rsqrt(jnp.mean(x * x, axis=-1, keepdims=True) + EPS) * w


def _const_spec(shape):
    nd = len(shape)
    return pl.BlockSpec(shape, lambda *_: (0,) * nd, pipeline_mode=pl.Buffered(1))


def _in_proj_kernel(h_ref, nw_ref, w_ref, cos_ref, sin_ref, o_ref, *, rope_groups):
    y = _rms(h_ref[...], nw_ref[...]).astype(BF16)
    n_out = o_ref.shape[1]
    lane = lax.broadcasted_iota(jnp.int32, (1, LANES), 1)
    first_half = (lane % HEAD_DIM) < (HEAD_DIM // 2)
    for c0 in range(0, n_out, PROJ_COLS):
        acc = jnp.dot(y, w_ref[:, c0:c0 + PROJ_COLS], preferred_element_type=F32)
        for g0 in range(0, PROJ_COLS, LANES):
            part = acc[:, g0:g0 + LANES]
            if (c0 + g0) // LANES < rope_groups:
                partner = jnp.where(first_half, pltpu.roll(part, LANES - HEAD_DIM // 2, 1),
                                    pltpu.roll(part, HEAD_DIM // 2, 1))
                part = part * cos_ref[...] + partner * sin_ref[...]
            o_ref[:, c0 + g0:c0 + g0 + LANES] = part.astype(BF16)


def _in_proj(h, nw, w, cos, sin, rope_groups, seq):
    n, d = h.shape
    n_out = w.shape[1]
    t = min(PROJ_TILE, seq)
    tiles_per_seq = seq // t
    return pl.pallas_call(
        functools.partial(_in_proj_kernel, rope_groups=rope_groups),
        grid=(n // t,),
        in_specs=[
            pl.BlockSpec((t, d), lambda i: (i, 0)),
            _const_spec((1, d)),
            _const_spec((d, n_out)),
            pl.BlockSpec((t, LANES), lambda i: (i % tiles_per_seq, 0)),
            pl.BlockSpec((t, LANES), lambda i: (i % tiles_per_seq, 0)),
        ],
        out_specs=pl.BlockSpec((t, n_out), lambda i: (i, 0)),
        out_shape=jax.ShapeDtypeStruct((n, n_out), BF16),
        compiler_params=pltpu.CompilerParams(
            dimension_semantics=("arbitrary",), vmem_limit_bytes=VMEM_LIMIT_BYTES),
        name="in_proj",
    )(h, nw, w, cos, sin)


def _band_attention_kernel(*refs, n_groups, n_prev, tq, has_sinks):
    q_ref, k_ref, v_ref, bias_ref = refs[:4]
    sink_ref = refs[4] if has_sinks else None
    o_ref, kpad, vtpad = refs[4 + has_sinks:]

    seq = q_ref.shape[1]
    pad = n_prev * CHUNK
    tk = pad + tq
    n_tiles = seq // tq

    kpad[0:pad, :] = jnp.zeros((pad, LANES), BF16)
    kpad[pad:pad + seq, :] = k_ref[0]
    vtpad[0:LANES, 0:pad] = jnp.zeros((LANES, pad), BF16)
    vtpad[0:LANES, pad:pad + seq] = v_ref[0].astype(F32).T.astype(BF16)
    vtpad[LANES:LANES + ONES_ROWS, :] = jnp.ones((ONES_ROWS, pad + seq), BF16)

    lane = lax.broadcasted_iota(jnp.int32, (1, LANES), 1)
    low = lane < HEAD_DIM
    key_idx = lax.broadcasted_iota(jnp.int32, (tk, 1), 0)
    zero = jnp.zeros((), BF16)

    def scores(ti):
        r0 = ti * tq
        q = q_ref[0, r0:r0 + tq, :]
        parts = []
        for j in range(n_groups):
            qj = q[:, j * LANES:(j + 1) * LANES]
            parts.append(jnp.where(low, qj, zero))
            parts.append(jnp.where(low, zero, qj))
        rows = jnp.concatenate(parts, axis=0)
        s = lax.dot_general(kpad[r0:r0 + tk, :], rows, (((1,), (1,)), ((), ())),
                            preferred_element_type=F32)
        s = s + bias_ref[0]
        if r0 < pad:
            s = jnp.where(key_idx >= pad - r0, s, -jnp.inf)
        return s

    def finish(ti, s):
        r0 = ti * tq
        m = jnp.max(s, axis=0, keepdims=True)
        if has_sinks:
            sk = sink_ref[...]
            m = jnp.maximum(m, sk)
        p = jnp.exp2(s - m).astype(BF16)
        o = jnp.dot(vtpad[:, r0:r0 + tk], p, preferred_element_type=F32)
        denom = o[LANES:LANES + 1]
        if has_sinks:
            denom = denom + jnp.exp2(sk - m)
        o = o[0:LANES] * (1.0 / denom)
        outs = []
        for j in range(n_groups):
            top = o[0:HEAD_DIM, 2 * j * tq:(2 * j + 1) * tq]
            bot = o[HEAD_DIM:LANES, (2 * j + 1) * tq:(2 * j + 2) * tq]
            outs.append(jnp.concatenate([top, bot], axis=0).T)
        o_ref[0, r0:r0 + tq, :] = jnp.concatenate(outs, axis=1).astype(BF16)

    s_next = scores(0)
    for ti in range(n_tiles):
        s_cur = s_next
        if ti + 1 < n_tiles:
            s_next = scores(ti + 1)
        finish(ti, s_cur)


def _band_mask(tq, n_prev):
    qc = np.arange(tq)[:, None] // CHUNK
    kc = np.arange(tq + n_prev * CHUNK)[None, :] // CHUNK
    return np.where((kc >= qc) & (kc <= qc + n_prev), 0.0, -np.inf).astype(np.float32)


def _band_attention(proj, bias, *, q_col, k_col, v_col, n_groups, groups_per_step, n_prev, tq, sinks=None):
    b, seq, _ = proj.shape
    n_steps = n_groups // groups_per_step
    per_head_kv = groups_per_step == 1
    qw = groups_per_step * LANES
    tk = tq + n_prev * CHUNK
    rows = groups_per_step * 2 * tq
    kv_map = (lambda col: (lambda j, i: (i, 0, col + j))) if per_head_kv else (lambda col: (lambda j, i: (i, 0, col)))
    in_specs = [
        pl.BlockSpec((1, seq, qw), lambda j, i: (i, 0, q_col // groups_per_step + j)),
        pl.BlockSpec((1, seq, LANES), kv_map(k_col)),
        pl.BlockSpec((1, seq, LANES), kv_map(v_col)),
        pl.BlockSpec((1, tk, rows), lambda j, i: (j, 0, 0)),
    ]
    args = [proj, proj, proj, bias]
    if sinks is not None:
        in_specs.append(pl.BlockSpec((1, rows), lambda j, i: (j, 0)))
        args.append(sinks)
    return pl.pallas_call(
        functools.partial(_band_attention_kernel, n_groups=groups_per_step, n_prev=n_prev, tq=tq,
                          has_sinks=sinks is not None),
        grid=(n_steps, b),
        in_specs=in_specs,
        out_specs=pl.BlockSpec((1, seq, qw), lambda j, i: (i, 0, j)),
        out_shape=jax.ShapeDtypeStruct((b, seq, n_groups * LANES), BF16),
        scratch_shapes=[pltpu.VMEM((seq + n_prev * CHUNK, LANES), BF16),
                        pltpu.VMEM((LANES + ONES_ROWS, seq + n_prev * CHUNK), BF16)],
        compiler_params=pltpu.CompilerParams(
            dimension_semantics=("arbitrary", "arbitrary"), vmem_limit_bytes=VMEM_LIMIT_BYTES),
        name="band_attention",
    )(*args)


def _hgrn_kernel(q_ref, f_ref, i_ref, g_ref, lbl_ref, nw_ref, o_ref, *, layer_j):
    seq = q_ref.shape[1]
    t = min(HGRN_TILE, seq)
    n_tiles = seq // t
    n_blk = t // B_BLOCK

    logits = lbl_ref[...]
    e = jnp.exp(logits - jnp.max(logits, axis=0, keepdims=True))
    sm = e / jnp.sum(e, axis=0, keepdims=True)
    lb = sm[0:1]
    for r in range(1, layer_j + 1):
        lb = lb + sm[r:r + 1]

    row = lax.broadcasted_iota(jnp.int32, (t, t), 0)
    col = lax.broadcasted_iota(jnp.int32, (t, t), 1)
    same_blk = (row // B_BLOCK) == (col // B_BLOCK)
    tril = jnp.logical_and(same_blk, col <= row)
    cum_mat = jnp.where(tril, 1.0, 0.0).astype(BF16)
    nw = nw_ref[...]

    def block_diag(x):
        cols = []
        for n in range(n_blk):
            pieces = [x[n * B_BLOCK:(n + 1) * B_BLOCK]]
            if n > 0:
                pieces.insert(0, jnp.zeros((n * B_BLOCK, x.shape[1]), x.dtype))
            if n < n_blk - 1:
                pieces.append(jnp.zeros(((n_blk - 1 - n) * B_BLOCK, x.shape[1]), x.dtype))
            cols.append(jnp.concatenate(pieces, axis=0))
        return jnp.concatenate(cols, axis=1)

    def tile_body(ti, state_t):
        r0 = pl.multiple_of(ti * t, t)
        fl = f_ref[0, pl.ds(r0, t), :].astype(F32)
        f = lb + (1.0 - lb) * jax.nn.sigmoid(fl)
        log_f = jnp.log(f)
        hi = log_f.astype(BF16)
        r1 = log_f - hi.astype(F32)
        mid = r1.astype(BF16)
        lo = (r1 - mid.astype(F32)).astype(BF16)
        parts = jnp.dot(cum_mat, jnp.concatenate([hi, mid, lo], axis=1), preferred_element_type=F32)
        cum = (parts[:, 0:LANES] + parts[:, LANES:2 * LANES]) + parts[:, 2 * LANES:3 * LANES]
        last = jnp.broadcast_to(cum.reshape(n_blk, B_BLOCK, B_KEY_DIM)[:, B_BLOCK - 1:B_BLOCK, :],
                                (n_blk, B_BLOCK, B_KEY_DIM)).reshape(t, B_KEY_DIM)
        k = 1.0 - f
        q = q_ref[0, pl.ds(r0, t), :].astype(F32)
        q_dec = (q * jnp.exp(cum)).astype(BF16)
        k_dec = (k * jnp.exp(-cum)).astype(BF16)
        k_end = (k * jnp.exp(last - cum)).astype(BF16)
        blk_decay = jnp.exp(last)
        v = i_ref[0, pl.ds(r0, t), :]
        s = lax.dot_general(q_dec, k_dec, (((1,), (1,)), ((), ())), preferred_element_type=F32)
        s = jnp.where(tril, s, 0.0)
        o_intra = jnp.dot(s.astype(BF16), v, preferred_element_type=F32)
        upd = lax.dot_general(v, block_diag(k_end), (((0,), (0,)), ((), ())), preferred_element_type=F32)
        prev_states = []
        for n in range(n_blk):
            prev_states.append(state_t.astype(BF16))
            state_t = (blk_decay[n * B_BLOCK:n * B_BLOCK + 1] * state_t
                       + upd[:, n * B_KEY_DIM:(n + 1) * B_KEY_DIM])
        o_inter = lax.dot_general(block_diag(q_dec), jnp.concatenate(prev_states, axis=1),
                                  (((1,), (1,)), ((), ())), preferred_element_type=F32)
        o = o_intra + o_inter
        o = _rms(o, nw)
        g = g_ref[0, pl.ds(r0, t), :].astype(F32)
        o_ref[0, pl.ds(r0, t), :] = (o * (g * jax.nn.sigmoid(g))).astype(BF16)
        return state_t

    lax.fori_loop(0, n_tiles, tile_body, jnp.zeros((B_VAL_DIM, B_KEY_DIM), F32), unroll=HGRN_UNROLL)


def _hgrn(proj, lb_logits, norm_w, *, q_col, f_col, i_col, g_col, layer_j):
    b, seq, _ = proj.shape
    n_layers = lb_logits.shape[0]

    def col_spec(col):
        return pl.BlockSpec((1, seq, LANES), lambda i, h: (i, 0, col + h))

    return pl.pallas_call(
        functools.partial(_hgrn_kernel, layer_j=layer_j),
        grid=(b, B_HEADS),
        in_specs=[col_spec(q_col), col_spec(f_col), col_spec(i_col), col_spec(g_col),
                  pl.BlockSpec((n_layers, LANES), lambda i, h: (0, h)),
                  pl.BlockSpec((1, LANES), lambda i, h: (0, h))],
        out_specs=pl.BlockSpec((1, seq, LANES), lambda i, h: (i, 0, h)),
        out_shape=jax.ShapeDtypeStruct((b, seq, B_V_DIM), BF16),
        compiler_params=pltpu.CompilerParams(
            dimension_semantics=("arbitrary", "arbitrary"), vmem_limit_bytes=VMEM_LIMIT_BYTES),
        name="hgrn2",
    )(proj, proj, proj, proj, lb_logits, norm_w)


def _out_ffn_kernel(*refs, n_mix, tiles_per_seq):
    mix_refs = refs[:n_mix]
    (h_ref, wout_ref, nw_ref, wup_ref, cw_ref, cb_ref, wdn_ref,
     hout_ref, g_scr, ubuf, carry) = refs[n_mix:]
    t = h_ref.shape[0]

    @pl.when(pl.program_id(0) % tiles_per_seq == 0)
    def _():
        carry[...] = jnp.zeros(carry.shape, F32)

    k0 = 0
    mixed = None
    for r in mix_refs:
        kw = r.shape[1]
        part = jnp.dot(r[...], wout_ref[k0:k0 + kw, :], preferred_element_type=F32)
        mixed = part if mixed is None else mixed + part
        k0 += kw
    h_mid = h_ref[...] + _rms(mixed, nw_ref[1:2])
    y = _rms(h_mid, nw_ref[2:3]).astype(BF16)

    sqrt_half = np.float32(np.sqrt(0.5))
    for c0 in range(0, D_FF, FFN_COLS):
        u = jnp.dot(y, wup_ref[:, c0:c0 + FFN_COLS], preferred_element_type=F32)
        v = jnp.dot(y, wup_ref[:, D_FF + c0:D_FF + c0 + FFN_COLS], preferred_element_type=F32)
        ubuf[0:CARRY_ROWS, :] = carry[:, c0:c0 + FFN_COLS]
        ubuf[CARRY_ROWS:CARRY_ROWS + t, :] = u
        carry[:, c0:c0 + FFN_COLS] = u[t - CARRY_ROWS:t]
        u1 = ubuf[CARRY_ROWS - 1:CARRY_ROWS - 1 + t, :]
        u2 = ubuf[CARRY_ROWS - 2:CARRY_ROWS - 2 + t, :]
        c = cb_ref[:, c0:c0 + FFN_COLS] + cw_ref[0:1, c0:c0 + FFN_COLS] * u2
        c = c + cw_ref[1:2, c0:c0 + FFN_COLS] * u1
        c = c + cw_ref[2:3, c0:c0 + FFN_COLS] * u
        gelu = 0.5 * c * (1.0 + lax.erf(c * sqrt_half))
        g_scr[:, c0:c0 + FFN_COLS] = (gelu * v).astype(BF16)

    down = jnp.dot(g_scr[...], wdn_ref[...], preferred_element_type=F32)
    hout_ref[...] = h_mid + _rms(down, nw_ref[3:4])


def _out_ffn(mix_list, h, w_out, nw, w_up, conv_w, conv_b, w_down, seq):
    n, d = h.shape
    t = min(PROJ_TILE, seq)
    tiles_per_seq = seq // t
    in_specs = [pl.BlockSpec((t, m.shape[1]), lambda i: (i, 0)) for m in mix_list]
    in_specs += [
        pl.BlockSpec((t, d), lambda i: (i, 0)),
        _const_spec(w_out.shape),
        _const_spec(nw.shape),
        _const_spec(w_up.shape),
        _const_spec(conv_w.shape),
        _const_spec(conv_b.shape),
        _const_spec(w_down.shape),
    ]
    return pl.pallas_call(
        functools.partial(_out_ffn_kernel, n_mix=len(mix_list), tiles_per_seq=tiles_per_seq),
        grid=(n // t,),
        in_specs=in_specs,
        out_specs=pl.BlockSpec((t, d), lambda i: (i, 0)),
        out_shape=jax.ShapeDtypeStruct((n, d), F32),
        scratch_shapes=[pltpu.VMEM((t, D_FF), BF16),
                        pltpu.VMEM((t + CARRY_ROWS, FFN_COLS), F32),
                        pltpu.VMEM((CARRY_ROWS, D_FF), F32)],
        compiler_params=pltpu.CompilerParams(
            dimension_semantics=("arbitrary",), vmem_limit_bytes=VMEM_LIMIT_BYTES),
        name="out_ffn",
    )(*mix_list, h, w_out, nw, w_up, conv_w, conv_b, w_down)


def _rope_tables(seq):
    inv_freq = ROPE_THETA ** (-jnp.arange(0, HEAD_DIM, 2, dtype=F32) / HEAD_DIM)
    ang = jnp.arange(seq, dtype=F32)[:, None] * inv_freq[None, :]
    cos, sin = jnp.cos(ang), jnp.sin(ang)
    reps = LANES // HEAD_DIM
    cos_t = jnp.tile(jnp.concatenate([cos, cos], axis=-1), (1, reps))
    sin_t = jnp.tile(jnp.concatenate([-sin, sin], axis=-1), (1, reps))
    return cos_t, sin_t


_A_HEAD_ORDER = tuple(h for j in range(A_Q_HEADS // 2) for h in (j, j + A_Q_HEADS // 2))


def _rel_bias(table):
    pad = C_PREV_CHUNKS * CHUNK
    band = pad + CHUNK
    period = band + CHUNK
    w = np.arange(period)
    diff = np.where(w < band, w, w - period)
    idx = np.clip(pad - diff, -REL_CLIP, REL_CLIP) + REL_CLIP
    vals = table[:, idx]
    skew = jnp.tile(vals, (1, CHUNK))[:, :CHUNK * (period - 1)].reshape(-1, CHUNK, period - 1)
    return skew[:, :, :band]


def _head_cols(order):
    return np.concatenate([np.arange(h * HEAD_DIM, (h + 1) * HEAD_DIM) for h in order])


def kernel(x, even_w_in, even_w_out, even_sinks, hgrn_lb_logits, hgrn_norm_w, odd_w_in, odd_w_out,
           odd_rel_bias, ffn_w_up, ffn_conv_w, ffn_conv_b, ffn_w_down, norm_w):
    b, seq, d = x.shape
    depth = norm_w.shape[0]
    scale = HEAD_DIM ** -0.5 * LOG2E
    cos_t, sin_t = _rope_tables(seq)
    a_cols = _head_cols(_A_HEAD_ORDER)

    h = x.reshape(b * seq, d)
    for layer in range(depth):
        nw = norm_w[layer]
        j = layer // 2
        if layer % 2 == 0:
            w_in = even_w_in[j]
            w_in = jnp.concatenate([w_in[:, :A_Q_DIM][:, a_cols] * scale, w_in[:, A_Q_DIM:]], axis=1).astype(BF16)
            w_out = jnp.concatenate([even_w_out[j][:A_Q_DIM][a_cols], even_w_out[j][A_Q_DIM:]], axis=0).astype(BF16)
            rope_groups = (A_Q_DIM + A_KV_DIM) // LANES
            proj = _in_proj(h, nw[0:1], w_in, cos_t, sin_t, rope_groups, seq).reshape(b, seq, EVEN_IN)
            tq = min(ATTN_A_TQ, seq)
            sinks = jnp.repeat(even_sinks[j][np.array(_A_HEAD_ORDER)].astype(F32) * LOG2E, tq)[None, :]
            n_groups = A_Q_DIM // LANES
            mask = jnp.asarray(np.tile(_band_mask(tq, A_PREV_CHUNKS).T, (1, 2 * n_groups))[None])
            oa = _band_attention(proj, mask, q_col=0, k_col=n_groups, v_col=n_groups + 1, n_groups=n_groups,
                                 groups_per_step=n_groups, n_prev=A_PREV_CHUNKS, tq=tq, sinks=sinks)
            c0 = (A_Q_DIM + 2 * A_KV_DIM) // LANES
            ob = _hgrn(proj, hgrn_lb_logits, hgrn_norm_w[j:j + 1],
                       q_col=c0, f_col=c0 + B_HEADS, i_col=c0 + 2 * B_HEADS, g_col=c0 + 3 * B_HEADS, layer_j=j)
            mix = [oa.reshape(b * seq, A_Q_DIM), ob.reshape(b * seq, B_V_DIM)]
        else:
            w_in = odd_w_in[j]
            w_in = jnp.concatenate([w_in[:, :C_DIM] * scale, w_in[:, C_DIM:]], axis=1).astype(BF16)
            w_out = odd_w_out[j].astype(BF16)
            proj = _in_proj(h, nw[0:1], w_in, cos_t, sin_t, 0, seq).reshape(b, seq, ODD_IN)
            pad = C_PREV_CHUNKS * CHUNK
            n_groups = C_DIM // LANES
            tq = min(ATTN_C_TQ, seq)
            n_qc = tq // CHUNK
            rel_bias = _rel_bias(odd_rel_bias[j].astype(F32) * LOG2E)
            bias = jnp.stack([jnp.pad(rel_bias, ((0, 0), (0, 0), (qc * CHUNK, (n_qc - 1 - qc) * CHUNK)),
                                      constant_values=-jnp.inf) for qc in range(n_qc)], axis=1)
            bias = jnp.swapaxes(bias.reshape(n_groups, 2 * tq, tq + pad), 1, 2)
            oc = _band_attention(proj, bias, q_col=0, k_col=n_groups, v_col=2 * n_groups, n_groups=n_groups,
                                 groups_per_step=1, n_prev=C_PREV_CHUNKS, tq=tq)
            mix = [oc.reshape(b * seq, C_DIM)]
        h = _out_ffn(mix, h, w_out, nw, ffn_w_up[layer].astype(BF16), ffn_conv_w[layer],
                     ffn_conv_b[layer][None, :], ffn_w_down[layer].astype(BF16), seq)
    return h.reshape(b, seq, d)
```

```python
import functools

import numpy as np
import jax
import jax.numpy as jnp
from jax import lax
from jax.experimental import pallas as pl
from jax.experimental.pallas import tpu as pltpu

F32 = jnp.float32
BF16 = jnp.bfloat16

D_MODEL = 1024
CHUNK = 64
HEAD_DIM = 64
ROPE_THETA = 10000.0
EPS = 1e-6
A_Q_HEADS = 8
A_KV_HEADS = 2
A_PREV_CHUNKS = 2
B_KEY_DIM = 128
B_VAL_DIM = 128
B_HEADS = 4
B_BLOCK = 32
C_HEADS = 16
C_PREV_CHUNKS = 8
REL_CLIP = 128
D_FF = 2816
CONV_WIDTH = 3
A_Q_DIM = A_Q_HEADS * HEAD_DIM
A_KV_DIM = A_KV_HEADS * HEAD_DIM
B_QK_DIM = B_HEADS * B_KEY_DIM
B_V_DIM = B_HEADS * B_VAL_DIM
EVEN_IN = A_Q_DIM + 2 * A_KV_DIM + 2 * B_QK_DIM + 2 * B_V_DIM
C_DIM = C_HEADS * HEAD_DIM
ODD_IN = 3 * C_DIM

LANES = 128
VMEM_LIMIT_BYTES = 56 * 1024 * 1024

PROJ_TILE = 512
PROJ_COLS = 256
PROJ_STREAMS = 2
FFN_COLS = 256
FFN_STREAMS = 2
HGRN_TILE = 256
ATTN_A_TQ = 128
ATTN_C_TQ = 256
ONES_ROWS = 16
LOG2E = float(np.log2(np.e))
CARRY_ROWS = 8


def _rms(x, w):
    return x * lax.rsqrt(jnp.mean(x * x, axis=-1, keepdims=True) + EPS) * w


def _const_spec(shape):
    nd = len(shape)
    return pl.BlockSpec(shape, lambda *_: (0,) * nd, pipeline_mode=pl.Buffered(1))


def _in_proj_kernel(h_ref, nw_ref, w_ref, cos_ref, sin_ref, o_ref, *, rope_groups):
    n_out = o_ref.shape[1]
    ts = h_ref.shape[0] // PROJ_STREAMS
    lane = lax.broadcasted_iota(jnp.int32, (1, LANES), 1)
    first_half = (lane % HEAD_DIM) < (HEAD_DIM // 2)
    cols = list(range(0, n_out, PROJ_COLS))

    def project(s, y, c0):
        rows = slice(s * ts, (s + 1) * ts)
        acc = jnp.dot(y, w_ref[:, c0:c0 + PROJ_COLS], preferred_element_type=F32)
        for g0 in range(0, PROJ_COLS, LANES):
            part = acc[:, g0:g0 + LANES]
            if (c0 + g0) // LANES < rope_groups:
                partner = jnp.where(first_half, pltpu.roll(part, LANES - HEAD_DIM // 2, 1),
                                    pltpu.roll(part, HEAD_DIM // 2, 1))
                part = part * cos_ref[rows, :] + partner * sin_ref[rows, :]
            o_ref[rows, c0 + g0:c0 + g0 + LANES] = part.astype(BF16)

    y = [None] * PROJ_STREAMS
    for step in range(len(cols) + PROJ_STREAMS - 1):
        for s in range(PROJ_STREAMS):
            ci = step - s
            if ci == 0:
                y[s] = _rms(h_ref[s * ts:(s + 1) * ts, :], nw_ref[...]).astype(BF16)
            if 0 <= ci < len(cols):
                project(s, y[s], cols[ci])


def _in_proj(h, nw, w, cos, sin, rope_groups, seq):
    n, d = h.shape
    n_out = w.shape[1]
    t = min(PROJ_TILE, seq)
    tiles_per_seq = seq // t
    return pl.pallas_call(
        functools.partial(_in_proj_kernel, rope_groups=rope_groups),
        grid=(n // t,),
        in_specs=[
            pl.BlockSpec((t, d), lambda i: (i, 0)),
            _const_spec((1, d)),
            _const_spec((d, n_out)),
            pl.BlockSpec((t, LANES), lambda i: (i % tiles_per_seq, 0)),
            pl.BlockSpec((t, LANES), lambda i: (i % tiles_per_seq, 0)),
        ],
        out_specs=pl.BlockSpec((t, n_out), lambda i: (i, 0)),
        out_shape=jax.ShapeDtypeStruct((n, n_out), BF16),
        compiler_params=pltpu.CompilerParams(
            dimension_semantics=("arbitrary",), vmem_limit_bytes=VMEM_LIMIT_BYTES),
        name="in_proj",
    )(h, nw, w, cos, sin)


def _band_attention_kernel(*refs, n_groups, n_prev, tq, has_sinks):
    q_ref, k_ref, v_ref, bias_ref = refs[:4]
    sink_ref = refs[4] if has_sinks else None
    o_ref, kpad, vtpad = refs[4 + has_sinks:]

    seq = q_ref.shape[1]
    pad = n_prev * CHUNK
    tk = pad + tq
    n_tiles = seq // tq

    kpad[0:pad, :] = jnp.zeros((pad, LANES), BF16)
    kpad[pad:pad + seq, :] = k_ref[0]
    vtpad[0:LANES, 0:pad] = jnp.zeros((LANES, pad), BF16)
    vtpad[0:LANES, pad:pad + seq] = v_ref[0].astype(F32).T.astype(BF16)
    vtpad[LANES:LANES + ONES_ROWS, :] = jnp.ones((ONES_ROWS, pad + seq), BF16)

    lane = lax.broadcasted_iota(jnp.int32, (1, LANES), 1)
    low = lane < HEAD_DIM
    key_idx = lax.broadcasted_iota(jnp.int32, (tk, 1), 0)
    zero = jnp.zeros((), BF16)

    def scores(ti):
        r0 = ti * tq
        q = q_ref[0, r0:r0 + tq, :]
        parts = []
        for j in range(n_groups):
            qj = q[:, j * LANES:(j + 1) * LANES]
            parts.append(jnp.where(low, qj, zero))
            parts.append(jnp.where(low, zero, qj))
        rows = jnp.concatenate(parts, axis=0)
        s = lax.dot_general(kpad[r0:r0 + tk, :], rows, (((1,), (1,)), ((), ())),
                            preferred_element_type=F32)
        s = s + bias_ref[0]
        if r0 < pad:
            s = jnp.where(key_idx >= pad - r0, s, -jnp.inf)
        return s

    def finish(ti, s):
        r0 = ti * tq
        m = jnp.max(s, axis=0, keepdims=True)
        if has_sinks:
            sk = sink_ref[...]
            m = jnp.maximum(m, sk)
        p = jnp.exp2(s - m).astype(BF16)
        o = jnp.dot(vtpad[:, r0:r0 + tk], p, preferred_element_type=F32)
        denom = o[LANES:LANES + 1]
        if has_sinks:
            denom = denom + jnp.exp2(sk - m)
        o = o[0:LANES] * (1.0 / denom)
        outs = []
        for j in range(n_groups):
            top = o[0:HEAD_DIM, 2 * j * tq:(2 * j + 1) * tq]
            bot = o[HEAD_DIM:LANES, (2 * j + 1) * tq:(2 * j + 2) * tq]
            outs.append(jnp.concatenate([top, bot], axis=0).T)
        o_ref[0, r0:r0 + tq, :] = jnp.concatenate(outs, axis=1).astype(BF16)

    s_next = scores(0)
    for ti in range(n_tiles):
        s_cur = s_next
        if ti + 1 < n_tiles:
            s_next = scores(ti + 1)
        finish(ti, s_cur)


def _band_mask(tq, n_prev):
    qc = np.arange(tq)[:, None] // CHUNK
    kc = np.arange(tq + n_prev * CHUNK)[None, :] // CHUNK
    return np.where((kc >= qc) & (kc <= qc + n_prev), 0.0, -np.inf).astype(np.float32)


def _band_attention(proj, bias, *, q_col, k_col, v_col, n_groups, groups_per_step, n_prev, tq, sinks=None):
    b, seq, _ = proj.shape
    n_steps = n_groups // groups_per_step
    per_head_kv = groups_per_step == 1
    qw = groups_per_step * LANES
    tk = tq + n_prev * CHUNK
    rows = groups_per_step * 2 * tq
    kv_map = (lambda col: (lambda j, i: (i, 0, col + j))) if per_head_kv else (lambda col: (lambda j, i: (i, 0, col)))
    in_specs = [
        pl.BlockSpec((1, seq, qw), lambda j, i: (i, 0, q_col // groups_per_step + j)),
        pl.BlockSpec((1, seq, LANES), kv_map(k_col)),
        pl.BlockSpec((1, seq, LANES), kv_map(v_col)),
        pl.BlockSpec((1, tk, rows), lambda j, i: (j, 0, 0)),
    ]
    args = [proj, proj, proj, bias]
    if sinks is not None:
        in_specs.append(pl.BlockSpec((1, rows), lambda j, i: (j, 0)))
        args.append(sinks)
    return pl.pallas_call(
        functools.partial(_band_attention_kernel, n_groups=groups_per_step, n_prev=n_prev, tq=tq,
                          has_sinks=sinks is not None),
        grid=(n_steps, b),
        in_specs=in_specs,
        out_specs=pl.BlockSpec((1, seq, qw), lambda j, i: (i, 0, j)),
        out_shape=jax.ShapeDtypeStruct((b, seq, n_groups * LANES), BF16),
        scratch_shapes=[pltpu.VMEM((seq + n_prev * CHUNK, LANES), BF16),
                        pltpu.VMEM((LANES + ONES_ROWS, seq + n_prev * CHUNK), BF16)],
        compiler_params=pltpu.CompilerParams(
            dimension_semantics=("arbitrary", "arbitrary"), vmem_limit_bytes=VMEM_LIMIT_BYTES),
        name="band_attention",
    )(*args)


def _hgrn_kernel(q_ref, f_ref, i_ref, g_ref, lbl_ref, nw_ref, o_ref, *, layer_j):
    seq = q_ref.shape[1]
    t = min(HGRN_TILE, seq)
    n_tiles = seq // t
    n_blk = t // B_BLOCK

    logits = lbl_ref[...]
    e = jnp.exp(logits - jnp.max(logits, axis=0, keepdims=True))
    sm = e / jnp.sum(e, axis=0, keepdims=True)
    lb = sm[0:1]
    for r in range(1, layer_j + 1):
        lb = lb + sm[r:r + 1]

    row = lax.broadcasted_iota(jnp.int32, (t, t), 0)
    col = lax.broadcasted_iota(jnp.int32, (t, t), 1)
    same_blk = (row // B_BLOCK) == (col // B_BLOCK)
    tril = jnp.logical_and(same_blk, col <= row)
    cum_mat = jnp.where(tril, 1.0, 0.0).astype(BF16)
    nw = nw_ref[...]

    def block_diag(x):
        cols = []
        for n in range(n_blk):
            pieces = [x[n * B_BLOCK:(n + 1) * B_BLOCK]]
            if n > 0:
                pieces.insert(0, jnp.zeros((n * B_BLOCK, x.shape[1]), x.dtype))
            if n < n_blk - 1:
                pieces.append(jnp.zeros(((n_blk - 1 - n) * B_BLOCK, x.shape[1]), x.dtype))
            cols.append(jnp.concatenate(pieces, axis=0))
        return jnp.concatenate(cols, axis=1)

    def state_free_part(ti):
        r0 = ti * t
        fl = f_ref[0, r0:r0 + t, :].astype(F32)
        f = lb + (1.0 - lb) * jax.nn.sigmoid(fl)
        log_f = jnp.log(f)
        hi = log_f.astype(BF16)
        r1 = log_f - hi.astype(F32)
        mid = r1.astype(BF16)
        lo = (r1 - mid.astype(F32)).astype(BF16)
        parts = jnp.dot(cum_mat, jnp.concatenate([hi, mid, lo], axis=1), preferred_element_type=F32)
        cum = (parts[:, 0:LANES] + parts[:, LANES:2 * LANES]) + parts[:, 2 * LANES:3 * LANES]
        last = jnp.broadcast_to(cum.reshape(n_blk, B_BLOCK, B_KEY_DIM)[:, B_BLOCK - 1:B_BLOCK, :],
                                (n_blk, B_BLOCK, B_KEY_DIM)).reshape(t, B_KEY_DIM)
        k = 1.0 - f
        q = q_ref[0, r0:r0 + t, :].astype(F32)
        q_dec = (q * jnp.exp(cum)).astype(BF16)
        k_dec = (k * jnp.exp(-cum)).astype(BF16)
        k_end = (k * jnp.exp(last - cum)).astype(BF16)
        blk_decay = jnp.exp(last)
        v = i_ref[0, r0:r0 + t, :]
        s = lax.dot_general(q_dec, k_dec, (((1,), (1,)), ((), ())), preferred_element_type=F32)
        s = jnp.where(tril, s, 0.0)
        o_intra = jnp.dot(s.astype(BF16), v, preferred_element_type=F32)
        upd = lax.dot_general(v, block_diag(k_end), (((0,), (0,)), ((), ())), preferred_element_type=F32)
        return q_dec, blk_decay, upd, o_intra

    def state_part(ti, state_t, q_dec, blk_decay, upd, o_intra):
        r0 = ti * t
        o_inter = []
        for n in range(n_blk):
            o_inter.append(lax.dot_general(q_dec[n * B_BLOCK:(n + 1) * B_BLOCK], state_t.astype(BF16),
                                           (((1,), (1,)), ((), ())), preferred_element_type=F32))
            state_t = (blk_decay[n * B_BLOCK:n * B_BLOCK + 1] * state_t
                       + upd[:, n * B_KEY_DIM:(n + 1) * B_KEY_DIM])
        o = o_intra + jnp.concatenate(o_inter, axis=0)
        o = _rms(o, nw)
        g = g_ref[0, r0:r0 + t, :].astype(F32)
        o_ref[0, r0:r0 + t, :] = (o * (g * jax.nn.sigmoid(g))).astype(BF16)
        return state_t

    state_t = jnp.zeros((B_VAL_DIM, B_KEY_DIM), F32)
    nxt = state_free_part(0)
    for ti in range(n_tiles):
        cur = nxt
        if ti + 1 < n_tiles:
            nxt = state_free_part(ti + 1)
        state_t = state_part(ti, state_t, *cur)


def _hgrn(proj, lb_logits, norm_w, *, q_col, f_col, i_col, g_col, layer_j):
    b, seq, _ = proj.shape
    n_layers = lb_logits.shape[0]

    def col_spec(col):
        return pl.BlockSpec((1, seq, LANES), lambda i, h: (i, 0, col + h))

    return pl.pallas_call(
        functools.partial(_hgrn_kernel, layer_j=layer_j),
        grid=(b, B_HEADS),
        in_specs=[col_spec(q_col), col_spec(f_col), col_spec(i_col), col_spec(g_col),
                  pl.BlockSpec((n_layers, LANES), lambda i, h: (0, h)),
                  pl.BlockSpec((1, LANES), lambda i, h: (0, h))],
        out_specs=pl.BlockSpec((1, seq, LANES), lambda i, h: (i, 0, h)),
        out_shape=jax.ShapeDtypeStruct((b, seq, B_V_DIM), BF16),
        compiler_params=pltpu.CompilerParams(
            dimension_semantics=("arbitrary", "arbitrary"), vmem_limit_bytes=VMEM_LIMIT_BYTES),
        name="hgrn2",
    )(proj, proj, proj, proj, lb_logits, norm_w)


def _out_ffn_kernel(*refs, n_mix, tiles_per_seq):
    mix_refs = refs[:n_mix]
    (h_ref, wout_ref, nw_ref, wup_ref, cw_ref, cb_ref, wdn_ref,
     hout_ref, g_scr, ubuf, carry) = refs[n_mix:]
    t = h_ref.shape[0]

    @pl.when(pl.program_id(0) % tiles_per_seq == 0)
    def _():
        carry[...] = jnp.zeros(carry.shape, F32)

    ts = t // FFN_STREAMS
    sqrt_half = np.float32(np.sqrt(0.5))
    cols = list(range(0, D_FF, FFN_COLS))

    def mix_dot(s):
        k0 = 0
        mixed = None
        for r in mix_refs:
            kw = r.shape[1]
            part = jnp.dot(r[s * ts:(s + 1) * ts, :], wout_ref[k0:k0 + kw, :], preferred_element_type=F32)
            mixed = part if mixed is None else mixed + part
            k0 += kw
        return mixed

    def mix_norm(s, mixed):
        h_mid = h_ref[s * ts:(s + 1) * ts, :] + _rms(mixed, nw_ref[1:2])
        return h_mid, _rms(h_mid, nw_ref[2:3]).astype(BF16)

    def ffn_chunk(s, y, c0):
        u = jnp.dot(y, wup_ref[:, c0:c0 + FFN_COLS], preferred_element_type=F32)
        v = jnp.dot(y, wup_ref[:, D_FF + c0:D_FF + c0 + FFN_COLS], preferred_element_type=F32)
        ubuf[s, 0:CARRY_ROWS, :] = carry[:, c0:c0 + FFN_COLS]
        ubuf[s, CARRY_ROWS:CARRY_ROWS + ts, :] = u
        carry[:, c0:c0 + FFN_COLS] = u[ts - CARRY_ROWS:ts]
        u1 = ubuf[s, CARRY_ROWS - 1:CARRY_ROWS - 1 + ts, :]
        u2 = ubuf[s, CARRY_ROWS - 2:CARRY_ROWS - 2 + ts, :]
        c = cb_ref[:, c0:c0 + FFN_COLS] + cw_ref[0:1, c0:c0 + FFN_COLS] * u2
        c = c + cw_ref[1:2, c0:c0 + FFN_COLS] * u1
        c = c + cw_ref[2:3, c0:c0 + FFN_COLS] * u
        gelu = 0.5 * c * (1.0 + lax.erf(c * sqrt_half))
        g_scr[s * ts:(s + 1) * ts, c0:c0 + FFN_COLS] = (gelu * v).astype(BF16)

    def down_dot(s):
        return jnp.dot(g_scr[s * ts:(s + 1) * ts, :], wdn_ref[...], preferred_element_type=F32)

    mixed = [mix_dot(s) for s in range(FFN_STREAMS)]
    h_mid, y = [None] * FFN_STREAMS, [None] * FFN_STREAMS
    for step in range(len(cols) + FFN_STREAMS - 1):
        for s in range(FFN_STREAMS):
            ci = step - s
            if ci == 0:
                h_mid[s], y[s] = mix_norm(s, mixed[s])
            if 0 <= ci < len(cols):
                ffn_chunk(s, y[s], cols[ci])
    down = [down_dot(s) for s in range(FFN_STREAMS)]
    for s in range(FFN_STREAMS):
        hout_ref[s * ts:(s + 1) * ts, :] = h_mid[s] + _rms(down[s], nw_ref[3:4])


def _out_ffn(mix_list, h, w_out, nw, w_up, conv_w, conv_b, w_down, seq):
    n, d = h.shape
    t = min(PROJ_TILE, seq)
    tiles_per_seq = seq // t
    in_specs = [pl.BlockSpec((t, m.shape[1]), lambda i: (i, 0)) for m in mix_list]
    in_specs += [
        pl.BlockSpec((t, d), lambda i: (i, 0)),
        _const_spec(w_out.shape),
        _const_spec(nw.shape),
        _const_spec(w_up.shape),
        _const_spec(conv_w.shape),
        _const_spec(conv_b.shape),
        _const_spec(w_down.shape),
    ]
    return pl.pallas_call(
        functools.partial(_out_ffn_kernel, n_mix=len(mix_list), tiles_per_seq=tiles_per_seq),
        grid=(n // t,),
        in_specs=in_specs,
        out_specs=pl.BlockSpec((t, d), lambda i: (i, 0)),
        out_shape=jax.ShapeDtypeStruct((n, d), F32),
        scratch_shapes=[pltpu.VMEM((t, D_FF), BF16),
                        pltpu.VMEM((FFN_STREAMS, t // FFN_STREAMS + CARRY_ROWS, FFN_COLS), F32),
                        pltpu.VMEM((CARRY_ROWS, D_FF), F32)],
        compiler_params=pltpu.CompilerParams(
            dimension_semantics=("arbitrary",), vmem_limit_bytes=VMEM_LIMIT_BYTES),
        name="out_ffn",
    )(*mix_list, h, w_out, nw, w_up, conv_w, conv_b, w_down)


def _rope_tables(seq):
    inv_freq = ROPE_THETA ** (-jnp.arange(0, HEAD_DIM, 2, dtype=F32) / HEAD_DIM)
    ang = jnp.arange(seq, dtype=F32)[:, None] * inv_freq[None, :]
    cos, sin = jnp.cos(ang), jnp.sin(ang)
    reps = LANES // HEAD_DIM
    cos_t = jnp.tile(jnp.concatenate([cos, cos], axis=-1), (1, reps))
    sin_t = jnp.tile(jnp.concatenate([-sin, sin], axis=-1), (1, reps))
    return cos_t, sin_t


_A_HEAD_ORDER = tuple(h for j in range(A_Q_HEADS // 2) for h in (j, j + A_Q_HEADS // 2))


def _rel_bias(table):
    pad = C_PREV_CHUNKS * CHUNK
    band = pad + CHUNK
    period = band + CHUNK
    w = np.arange(period)
    diff = np.where(w < band, w, w - period)
    idx = np.clip(pad - diff, -REL_CLIP, REL_CLIP) + REL_CLIP
    vals = table[:, idx]
    skew = jnp.tile(vals, (1, CHUNK))[:, :CHUNK * (period - 1)].reshape(-1, CHUNK, period - 1)
    return skew[:, :, :band]


def _head_cols(order):
    return np.concatenate([np.arange(h * HEAD_DIM, (h + 1) * HEAD_DIM) for h in order])


def kernel(x, even_w_in, even_w_out, even_sinks, hgrn_lb_logits, hgrn_norm_w, odd_w_in, odd_w_out,
           odd_rel_bias, ffn_w_up, ffn_conv_w, ffn_conv_b, ffn_w_down, norm_w):
    b, seq, d = x.shape
    depth = norm_w.shape[0]
    scale = HEAD_DIM ** -0.5 * LOG2E
    cos_t, sin_t = _rope_tables(seq)
    a_cols = _head_cols(_A_HEAD_ORDER)

    h = x.reshape(b * seq, d)
    for layer in range(depth):
        nw = norm_w[layer]
        j = layer // 2
        if layer % 2 == 0:
            w_in = even_w_in[j]
            w_in = jnp.concatenate([w_in[:, :A_Q_DIM][:, a_cols] * scale, w_in[:, A_Q_DIM:]], axis=1).astype(BF16)
            w_out = jnp.concatenate([even_w_out[j][:A_Q_DIM][a_cols], even_w_out[j][A_Q_DIM:]], axis=0).astype(BF16)
            rope_groups = (A_Q_DIM + A_KV_DIM) // LANES
            proj = _in_proj(h, nw[0:1], w_in, cos_t, sin_t, rope_groups, seq).reshape(b, seq, EVEN_IN)
            tq = min(ATTN_A_TQ, seq)
            sinks = jnp.repeat(even_sinks[j][np.array(_A_HEAD_ORDER)].astype(F32) * LOG2E, tq)[None, :]
            n_groups = A_Q_DIM // LANES
            mask = jnp.asarray(np.tile(_band_mask(tq, A_PREV_CHUNKS).T, (1, 2 * n_groups))[None])
            oa = _band_attention(proj, mask, q_col=0, k_col=n_groups, v_col=n_groups + 1, n_groups=n_groups,
                                 groups_per_step=n_groups, n_prev=A_PREV_CHUNKS, tq=tq, sinks=sinks)
            c0 = (A_Q_DIM + 2 * A_KV_DIM) // LANES
            ob = _hgrn(proj, hgrn_lb_logits, hgrn_norm_w[j:j + 1],
                       q_col=c0, f_col=c0 + B_HEADS, i_col=c0 + 2 * B_HEADS, g_col=c0 + 3 * B_HEADS, layer_j=j)
            mix = [oa.reshape(b * seq, A_Q_DIM), ob.reshape(b * seq, B_V_DIM)]
        else:
            w_in = odd_w_in[j]
            w_in = jnp.concatenate([w_in[:, :C_DIM] * scale, w_in[:, C_DIM:]], axis=1).astype(BF16)
            w_out = odd_w_out[j].astype(BF16)
            proj = _in_proj(h, nw[0:1], w_in, cos_t, sin_t, 0, seq).reshape(b, seq, ODD_IN)
            pad = C_PREV_CHUNKS * CHUNK
            n_groups = C_DIM // LANES
            tq = min(ATTN_C_TQ, seq)
            n_qc = tq // CHUNK
            rel_bias = _rel_bias(odd_rel_bias[j].astype(F32) * LOG2E)
            bias = jnp.stack([jnp.pad(rel_bias, ((0, 0), (0, 0), (qc * CHUNK, (n_qc - 1 - qc) * CHUNK)),
                                      constant_values=-jnp.inf) for qc in range(n_qc)], axis=1)
            bias = jnp.swapaxes(bias.reshape(n_groups, 2 * tq, tq + pad), 1, 2)
            oc = _band_attention(proj, bias, q_col=0, k_col=n_groups, v_col=2 * n_groups, n_groups=n_groups,
                                 groups_per_step=1, n_prev=C_PREV_CHUNKS, tq=tq)
            mix = [oc.reshape(b * seq, C_DIM)]
        h = _out_ffn(mix, h, w_out, nw, ffn_w_up[layer].astype(BF16), ffn_conv_w[layer],
                     ffn_conv_b[layer][None, :], ffn_w_down[layer].astype(BF16), seq)
    return h.reshape(b, seq, d)
```

```python
import functools

import numpy as np
import jax
import jax.numpy as jnp
from jax import lax
from jax.experimental import pallas as pl
from jax.experimental.pallas import tpu as pltpu

F32 = jnp.float32
BF16 = jnp.bfloat16

D_MODEL = 1024
CHUNK = 64
HEAD_DIM = 64
ROPE_THETA = 10000.0
EPS = 1e-6
A_Q_HEADS = 8
A_KV_HEADS = 2
A_PREV_CHUNKS = 2
B_KEY_DIM = 128
B_VAL_DIM = 128
B_HEADS = 4
B_BLOCK = 32
C_HEADS = 16
C_PREV_CHUNKS = 8
REL_CLIP = 128
D_FF = 2816
CONV_WIDTH = 3
A_Q_DIM = A_Q_HEADS * HEAD_DIM
A_KV_DIM = A_KV_HEADS * HEAD_DIM
B_QK_DIM = B_HEADS * B_KEY_DIM
B_V_DIM = B_HEADS * B_VAL_DIM
EVEN_IN = A_Q_DIM + 2 * A_KV_DIM + 2 * B_QK_DIM + 2 * B_V_DIM
C_DIM = C_HEADS * HEAD_DIM
ODD_IN = 3 * C_DIM

LANES = 128
VMEM_LIMIT_BYTES = 56 * 1024 * 1024

PROJ_TILE = 512
PROJ_COLS = 256
PROJ_STREAMS = 2
FFN_COLS = 256
FFN_STREAMS = 2
HGRN_TILE = 256
ATTN_A_TQ = 128
ATTN_C_TQ = 256
ONES_ROWS = 16
LOG2E = float(np.log2(np.e))
CARRY_ROWS = 8


def _rms(x, w):
    return x * lax.rsqrt(jnp.mean(x * x, axis=-1, keepdims=True) + EPS) * w


def _const_spec(shape):
    nd = len(shape)
    return pl.BlockSpec(shape, lambda *_: (0,) * nd, pipeline_mode=pl.Buffered(1))


def _in_proj_kernel(h_ref, nw_ref, w_ref, cos_ref, sin_ref, o_ref, *, rope_groups):
    n_out = w_ref.shape[1]
    ts = h_ref.shape[0] // PROJ_STREAMS
    lane = lax.broadcasted_iota(jnp.int32, (1, LANES), 1)
    first_half = (lane % HEAD_DIM) < (HEAD_DIM // 2)
    cols = list(range(0, n_out, PROJ_COLS))

    def project(s, y, c0):
        rows = slice(s * ts, (s + 1) * ts)
        acc = jnp.dot(y, w_ref[:, c0:c0 + PROJ_COLS], preferred_element_type=F32)
        for g0 in range(0, PROJ_COLS, LANES):
            part = acc[:, g0:g0 + LANES]
            if (c0 + g0) // LANES < rope_groups:
                partner = jnp.where(first_half, pltpu.roll(part, LANES - HEAD_DIM // 2, 1),
                                    pltpu.roll(part, HEAD_DIM // 2, 1))
                part = part * cos_ref[rows, :] + partner * sin_ref[rows, :]
            o_ref[(c0 + g0) // LANES, rows, :] = part.astype(BF16)

    y = [None] * PROJ_STREAMS
    for step in range(len(cols) + PROJ_STREAMS - 1):
        for s in range(PROJ_STREAMS):
            ci = step - s
            if ci == 0:
                y[s] = _rms(h_ref[s * ts:(s + 1) * ts, :], nw_ref[...]).astype(BF16)
            if 0 <= ci < len(cols):
                project(s, y[s], cols[ci])


def _in_proj(h, nw, w, cos, sin, rope_groups, seq):
    n, d = h.shape
    n_out = w.shape[1]
    n_grp = n_out // LANES
    t = min(PROJ_TILE, seq)
    tiles_per_seq = seq // t
    return pl.pallas_call(
        functools.partial(_in_proj_kernel, rope_groups=rope_groups),
        grid=(n // t,),
        in_specs=[
            pl.BlockSpec((t, d), lambda i: (i, 0)),
            _const_spec((1, d)),
            _const_spec((d, n_out)),
            pl.BlockSpec((t, LANES), lambda i: (i % tiles_per_seq, 0)),
            pl.BlockSpec((t, LANES), lambda i: (i % tiles_per_seq, 0)),
        ],
        out_specs=pl.BlockSpec((n_grp, t, LANES), lambda i: (0, i, 0)),
        out_shape=jax.ShapeDtypeStruct((n_grp, n, LANES), BF16),
        compiler_params=pltpu.CompilerParams(
            dimension_semantics=("arbitrary",), vmem_limit_bytes=VMEM_LIMIT_BYTES),
        name="in_proj",
    )(h, nw, w, cos, sin)


def _band_attention_kernel(*refs, n_groups, n_prev, tq, has_sinks):
    q_ref, k_ref, v_ref, bias_ref = refs[:4]
    sink_ref = refs[4] if has_sinks else None
    o_ref, kpad, vtpad = refs[4 + has_sinks:]

    seq = q_ref.shape[2]
    pad = n_prev * CHUNK
    tk = pad + tq
    n_tiles = seq // tq

    kpad[0:pad, :] = jnp.zeros((pad, LANES), BF16)
    kpad[pad:pad + seq, :] = k_ref[0, 0]
    vtpad[0:LANES, 0:pad] = jnp.zeros((LANES, pad), BF16)
    vtpad[0:LANES, pad:pad + seq] = v_ref[0, 0].astype(F32).T.astype(BF16)
    vtpad[LANES:LANES + ONES_ROWS, :] = jnp.ones((ONES_ROWS, pad + seq), BF16)

    lane = lax.broadcasted_iota(jnp.int32, (1, LANES), 1)
    low = lane < HEAD_DIM
    key_idx = lax.broadcasted_iota(jnp.int32, (tk, 1), 0)
    zero = jnp.zeros((), BF16)

    def scores(ti):
        r0 = ti * tq
        parts = []
        for j in range(n_groups):
            qj = q_ref[j, 0, r0:r0 + tq, :]
            parts.append(jnp.where(low, qj, zero))
            parts.append(jnp.where(low, zero, qj))
        rows = jnp.concatenate(parts, axis=0)
        s = lax.dot_general(kpad[r0:r0 + tk, :], rows, (((1,), (1,)), ((), ())),
                            preferred_element_type=F32)
        s = s + bias_ref[0]
        if r0 < pad:
            s = jnp.where(key_idx >= pad - r0, s, -jnp.inf)
        return s

    def softmax(s):
        m = jnp.max(s, axis=0, keepdims=True)
        if has_sinks:
            m = jnp.maximum(m, sink_ref[...])
        return jnp.exp2(s - m).astype(BF16), m

    def finish(ti, p, m):
        r0 = ti * tq
        o = jnp.dot(vtpad[:, r0:r0 + tk], p, preferred_element_type=F32)
        denom = o[LANES:LANES + 1]
        if has_sinks:
            denom = denom + jnp.exp2(sink_ref[...] - m)
        o = o[0:LANES] * (1.0 / denom)
        for j in range(n_groups):
            top = o[0:HEAD_DIM, 2 * j * tq:(2 * j + 1) * tq]
            bot = o[HEAD_DIM:LANES, (2 * j + 1) * tq:(2 * j + 2) * tq]
            o_ref[j, 0, r0:r0 + tq, :] = jnp.concatenate([top, bot], axis=0).T.astype(BF16)

    s_buf, p_buf = {}, {}
    for step in range(n_tiles + 2):
        if step < n_tiles:
            s_buf[step] = scores(step)
        if 0 <= step - 1 < n_tiles:
            p_buf[step - 1] = softmax(s_buf.pop(step - 1))
        if 0 <= step - 2 < n_tiles:
            finish(step - 2, *p_buf.pop(step - 2))


def _band_mask(tq, n_prev):
    qc = np.arange(tq)[:, None] // CHUNK
    kc = np.arange(tq + n_prev * CHUNK)[None, :] // CHUNK
    return np.where((kc >= qc) & (kc <= qc + n_prev), 0.0, -np.inf).astype(np.float32)


def _band_attention(proj, bias, *, q_col, k_col, v_col, n_groups, groups_per_step, n_prev, tq, sinks=None):
    _, b, seq, _ = proj.shape
    n_steps = n_groups // groups_per_step
    per_head_kv = groups_per_step == 1
    tk = tq + n_prev * CHUNK
    rows = groups_per_step * 2 * tq
    kv_map = ((lambda col: (lambda j, i: (col + j, i, 0, 0))) if per_head_kv
              else (lambda col: (lambda j, i: (col, i, 0, 0))))
    in_specs = [
        pl.BlockSpec((groups_per_step, 1, seq, LANES), lambda j, i: (q_col // groups_per_step + j, i, 0, 0)),
        pl.BlockSpec((1, 1, seq, LANES), kv_map(k_col)),
        pl.BlockSpec((1, 1, seq, LANES), kv_map(v_col)),
        pl.BlockSpec((1, tk, rows), lambda j, i: (j, 0, 0)),
    ]
    args = [proj, proj, proj, bias]
    if sinks is not None:
        in_specs.append(pl.BlockSpec((1, rows), lambda j, i: (j, 0)))
        args.append(sinks)
    return pl.pallas_call(
        functools.partial(_band_attention_kernel, n_groups=groups_per_step, n_prev=n_prev, tq=tq,
                          has_sinks=sinks is not None),
        grid=(n_steps, b),
        in_specs=in_specs,
        out_specs=pl.BlockSpec((groups_per_step, 1, seq, LANES), lambda j, i: (j, i, 0, 0)),
        out_shape=jax.ShapeDtypeStruct((n_groups, b, seq, LANES), BF16),
        scratch_shapes=[pltpu.VMEM((seq + n_prev * CHUNK, LANES), BF16),
                        pltpu.VMEM((LANES + ONES_ROWS, seq + n_prev * CHUNK), BF16)],
        compiler_params=pltpu.CompilerParams(
            dimension_semantics=("arbitrary", "arbitrary"), vmem_limit_bytes=VMEM_LIMIT_BYTES),
        name="band_attention",
    )(*args)


def _hgrn_kernel(q_ref, f_ref, i_ref, g_ref, lbl_ref, nw_ref, o_ref, *, layer_j):
    seq = q_ref.shape[0]
    t = min(HGRN_TILE, seq)
    n_tiles = seq // t
    n_blk = t // B_BLOCK

    logits = lbl_ref[...]
    e = jnp.exp(logits - jnp.max(logits, axis=0, keepdims=True))
    sm = e / jnp.sum(e, axis=0, keepdims=True)
    lb = sm[0:1]
    for r in range(1, layer_j + 1):
        lb = lb + sm[r:r + 1]

    row = lax.broadcasted_iota(jnp.int32, (t, t), 0)
    col = lax.broadcasted_iota(jnp.int32, (t, t), 1)
    same_blk = (row // B_BLOCK) == (col // B_BLOCK)
    tril = jnp.logical_and(same_blk, col <= row)
    cum_mat = jnp.where(tril, 1.0, 0.0).astype(BF16)
    nw = nw_ref[...]

    def block_diag(x):
        cols = []
        for n in range(n_blk):
            pieces = [x[n * B_BLOCK:(n + 1) * B_BLOCK]]
            if n > 0:
                pieces.insert(0, jnp.zeros((n * B_BLOCK, x.shape[1]), x.dtype))
            if n < n_blk - 1:
                pieces.append(jnp.zeros(((n_blk - 1 - n) * B_BLOCK, x.shape[1]), x.dtype))
            cols.append(jnp.concatenate(pieces, axis=0))
        return jnp.concatenate(cols, axis=1)

    def state_free_part(ti):
        r0 = ti * t
        fl = f_ref[r0:r0 + t, :].astype(F32)
        f = lb + (1.0 - lb) * jax.nn.sigmoid(fl)
        log_f = jnp.log(f)
        hi = log_f.astype(BF16)
        r1 = log_f - hi.astype(F32)
        mid = r1.astype(BF16)
        lo = (r1 - mid.astype(F32)).astype(BF16)
        parts = jnp.dot(cum_mat, jnp.concatenate([hi, mid, lo], axis=1), preferred_element_type=F32)
        cum = (parts[:, 0:LANES] + parts[:, LANES:2 * LANES]) + parts[:, 2 * LANES:3 * LANES]
        last = jnp.broadcast_to(cum.reshape(n_blk, B_BLOCK, B_KEY_DIM)[:, B_BLOCK - 1:B_BLOCK, :],
                                (n_blk, B_BLOCK, B_KEY_DIM)).reshape(t, B_KEY_DIM)
        k = 1.0 - f
        q = q_ref[r0:r0 + t, :].astype(F32)
        q_dec = (q * jnp.exp(cum)).astype(BF16)
        k_dec = (k * jnp.exp(-cum)).astype(BF16)
        k_end = (k * jnp.exp(last - cum)).astype(BF16)
        blk_decay = jnp.exp(last)
        v = i_ref[r0:r0 + t, :]
        s = lax.dot_general(q_dec, k_dec, (((1,), (1,)), ((), ())), preferred_element_type=F32)
        s = jnp.where(tril, s, 0.0)
        o_intra = jnp.dot(s.astype(BF16), v, preferred_element_type=F32)
        upd = lax.dot_general(v, block_diag(k_end), (((0,), (0,)), ((), ())), preferred_element_type=F32)
        return q_dec, blk_decay, upd, o_intra

    def state_part(ti, state_t, q_dec, blk_decay, upd, o_intra):
        r0 = ti * t
        o_inter = []
        for n in range(n_blk):
            o_inter.append(lax.dot_general(q_dec[n * B_BLOCK:(n + 1) * B_BLOCK], state_t.astype(BF16),
                                           (((1,), (1,)), ((), ())), preferred_element_type=F32))
            state_t = (blk_decay[n * B_BLOCK:n * B_BLOCK + 1] * state_t
                       + upd[:, n * B_KEY_DIM:(n + 1) * B_KEY_DIM])
        o = o_intra + jnp.concatenate(o_inter, axis=0)
        o = _rms(o, nw)
        g = g_ref[r0:r0 + t, :].astype(F32)
        o_ref[r0:r0 + t, :] = (o * (g * jax.nn.sigmoid(g))).astype(BF16)
        return state_t

    state_t = jnp.zeros((B_VAL_DIM, B_KEY_DIM), F32)
    nxt = state_free_part(0)
    for ti in range(n_tiles):
        cur = nxt
        if ti + 1 < n_tiles:
            nxt = state_free_part(ti + 1)
        state_t = state_part(ti, state_t, *cur)


def _hgrn(proj, lb_logits, norm_w, *, q_col, f_col, i_col, g_col, layer_j):
    _, b, seq, _ = proj.shape
    n_layers = lb_logits.shape[0]

    def col_spec(col):
        return pl.BlockSpec((None, None, seq, LANES), lambda i, h: (col + h, i, 0, 0))

    return pl.pallas_call(
        functools.partial(_hgrn_kernel, layer_j=layer_j),
        grid=(b, B_HEADS),
        in_specs=[col_spec(q_col), col_spec(f_col), col_spec(i_col), col_spec(g_col),
                  pl.BlockSpec((n_layers, LANES), lambda i, h: (0, h)),
                  pl.BlockSpec((1, LANES), lambda i, h: (0, h))],
        out_specs=pl.BlockSpec((None, None, seq, LANES), lambda i, h: (h, i, 0, 0)),
        out_shape=jax.ShapeDtypeStruct((B_HEADS, b, seq, LANES), BF16),
        compiler_params=pltpu.CompilerParams(
            dimension_semantics=("arbitrary", "arbitrary"), vmem_limit_bytes=VMEM_LIMIT_BYTES),
        name="hgrn2",
    )(proj, proj, proj, proj, lb_logits, norm_w)


def _out_ffn_kernel(*refs, n_mix, tiles_per_seq):
    mix_refs = refs[:n_mix]
    (h_ref, wout_ref, nw_ref, wup_ref, cw_ref, cb_ref, wdn_ref,
     hout_ref, g_scr, ubuf, carry) = refs[n_mix:]
    t = h_ref.shape[0]

    @pl.when(pl.program_id(0) % tiles_per_seq == 0)
    def _():
        carry[...] = jnp.zeros(carry.shape, F32)

    ts = t // FFN_STREAMS
    sqrt_half = np.float32(np.sqrt(0.5))
    cols = list(range(0, D_FF, FFN_COLS))

    def mix_dot(s):
        mix = jnp.concatenate([r[g, s * ts:(s + 1) * ts, :] for r in mix_refs for g in range(r.shape[0])], axis=1)
        return jnp.dot(mix, wout_ref[...], preferred_element_type=F32)

    def mix_norm(s, mixed):
        h_mid = h_ref[s * ts:(s + 1) * ts, :] + _rms(mixed, nw_ref[1:2])
        return h_mid, _rms(h_mid, nw_ref[2:3]).astype(BF16)

    def ffn_chunk(s, y, c0):
        u = jnp.dot(y, wup_ref[:, c0:c0 + FFN_COLS], preferred_element_type=F32)
        v = jnp.dot(y, wup_ref[:, D_FF + c0:D_FF + c0 + FFN_COLS], preferred_element_type=F32)
        ubuf[s, 0:CARRY_ROWS, :] = carry[:, c0:c0 + FFN_COLS]
        ubuf[s, CARRY_ROWS:CARRY_ROWS + ts, :] = u
        carry[:, c0:c0 + FFN_COLS] = u[ts - CARRY_ROWS:ts]
        u1 = ubuf[s, CARRY_ROWS - 1:CARRY_ROWS - 1 + ts, :]
        u2 = ubuf[s, CARRY_ROWS - 2:CARRY_ROWS - 2 + ts, :]
        c = cb_ref[:, c0:c0 + FFN_COLS] + cw_ref[0:1, c0:c0 + FFN_COLS] * u2
        c = c + cw_ref[1:2, c0:c0 + FFN_COLS] * u1
        c = c + cw_ref[2:3, c0:c0 + FFN_COLS] * u
        gelu = 0.5 * c * (1.0 + lax.erf(c * sqrt_half))
        g_scr[s * ts:(s + 1) * ts, c0:c0 + FFN_COLS] = (gelu * v).astype(BF16)

    def down_dot(s):
        return jnp.dot(g_scr[s * ts:(s + 1) * ts, :], wdn_ref[...], preferred_element_type=F32)

    mixed = [mix_dot(s) for s in range(FFN_STREAMS)]
    h_mid, y = [None] * FFN_STREAMS, [None] * FFN_STREAMS
    for step in range(len(cols) + FFN_STREAMS - 1):
        for s in range(FFN_STREAMS):
            ci = step - s
            if ci == 0:
                h_mid[s], y[s] = mix_norm(s, mixed[s])
            if 0 <= ci < len(cols):
                ffn_chunk(s, y[s], cols[ci])
    down = [down_dot(s) for s in range(FFN_STREAMS)]
    for s in range(FFN_STREAMS):
        hout_ref[s * ts:(s + 1) * ts, :] = h_mid[s] + _rms(down[s], nw_ref[3:4])


def _out_ffn(mix_list, h, w_out, nw, w_up, conv_w, conv_b, w_down, seq):
    n, d = h.shape
    t = min(PROJ_TILE, seq)
    tiles_per_seq = seq // t
    in_specs = [pl.BlockSpec((m.shape[0], t, LANES), lambda i: (0, i, 0)) for m in mix_list]
    in_specs += [
        pl.BlockSpec((t, d), lambda i: (i, 0)),
        _const_spec(w_out.shape),
        _const_spec(nw.shape),
        _const_spec(w_up.shape),
        _const_spec(conv_w.shape),
        _const_spec(conv_b.shape),
        _const_spec(w_down.shape),
    ]
    return pl.pallas_call(
        functools.partial(_out_ffn_kernel, n_mix=len(mix_list), tiles_per_seq=tiles_per_seq),
        grid=(n // t,),
        in_specs=in_specs,
        out_specs=pl.BlockSpec((t, d), lambda i: (i, 0)),
        out_shape=jax.ShapeDtypeStruct((n, d), F32),
        scratch_shapes=[pltpu.VMEM((t, D_FF), BF16),
                        pltpu.VMEM((FFN_STREAMS, t // FFN_STREAMS + CARRY_ROWS, FFN_COLS), F32),
                        pltpu.VMEM((CARRY_ROWS, D_FF), F32)],
        compiler_params=pltpu.CompilerParams(
            dimension_semantics=("arbitrary",), vmem_limit_bytes=VMEM_LIMIT_BYTES),
        name="out_ffn",
    )(*mix_list, h, w_out, nw, w_up, conv_w, conv_b, w_down)


def _rope_tables(seq):
    inv_freq = ROPE_THETA ** (-jnp.arange(0, HEAD_DIM, 2, dtype=F32) / HEAD_DIM)
    ang = jnp.arange(seq, dtype=F32)[:, None] * inv_freq[None, :]
    cos, sin = jnp.cos(ang), jnp.sin(ang)
    reps = LANES // HEAD_DIM
    cos_t = jnp.tile(jnp.concatenate([cos, cos], axis=-1), (1, reps))
    sin_t = jnp.tile(jnp.concatenate([-sin, sin], axis=-1), (1, reps))
    return cos_t, sin_t


_A_HEAD_ORDER = tuple(h for j in range(A_Q_HEADS // 2) for h in (j, j + A_Q_HEADS // 2))


def _rel_bias(table):
    pad = C_PREV_CHUNKS * CHUNK
    band = pad + CHUNK
    period = band + CHUNK
    w = np.arange(period)
    diff = np.where(w < band, w, w - period)
    idx = np.clip(pad - diff, -REL_CLIP, REL_CLIP) + REL_CLIP
    vals = table[:, idx]
    skew = jnp.tile(vals, (1, CHUNK))[:, :CHUNK * (period - 1)].reshape(-1, CHUNK, period - 1)
    return skew[:, :, :band]


def _head_cols(order):
    return np.concatenate([np.arange(h * HEAD_DIM, (h + 1) * HEAD_DIM) for h in order])


def kernel(x, even_w_in, even_w_out, even_sinks, hgrn_lb_logits, hgrn_norm_w, odd_w_in, odd_w_out,
           odd_rel_bias, ffn_w_up, ffn_conv_w, ffn_conv_b, ffn_w_down, norm_w):
    b, seq, d = x.shape
    depth = norm_w.shape[0]
    scale = HEAD_DIM ** -0.5 * LOG2E
    cos_t, sin_t = _rope_tables(seq)
    a_cols = _head_cols(_A_HEAD_ORDER)

    h = x.reshape(b * seq, d)
    for layer in range(depth):
        nw = norm_w[layer]
        j = layer // 2
        if layer % 2 == 0:
            w_in = even_w_in[j]
            w_in = jnp.concatenate([w_in[:, :A_Q_DIM][:, a_cols] * scale, w_in[:, A_Q_DIM:]], axis=1).astype(BF16)
            w_out = jnp.concatenate([even_w_out[j][:A_Q_DIM][a_cols], even_w_out[j][A_Q_DIM:]], axis=0).astype(BF16)
            rope_groups = (A_Q_DIM + A_KV_DIM) // LANES
            proj = _in_proj(h, nw[0:1], w_in, cos_t, sin_t, rope_groups, seq).reshape(-1, b, seq, LANES)
            tq = min(ATTN_A_TQ, seq)
            sinks = jnp.repeat(even_sinks[j][np.array(_A_HEAD_ORDER)].astype(F32) * LOG2E, tq)[None, :]
            n_groups = A_Q_DIM // LANES
            mask = jnp.asarray(np.tile(_band_mask(tq, A_PREV_CHUNKS).T, (1, 2 * n_groups))[None])
            oa = _band_attention(proj, mask, q_col=0, k_col=n_groups, v_col=n_groups + 1, n_groups=n_groups,
                                 groups_per_step=n_groups, n_prev=A_PREV_CHUNKS, tq=tq, sinks=sinks)
            c0 = (A_Q_DIM + 2 * A_KV_DIM) // LANES
            ob = _hgrn(proj, hgrn_lb_logits, hgrn_norm_w[j:j + 1],
                       q_col=c0, f_col=c0 + B_HEADS, i_col=c0 + 2 * B_HEADS, g_col=c0 + 3 * B_HEADS, layer_j=j)
            mix = [oa.reshape(-1, b * seq, LANES), ob.reshape(-1, b * seq, LANES)]
        else:
            w_in = odd_w_in[j]
            w_in = jnp.concatenate([w_in[:, :C_DIM] * scale, w_in[:, C_DIM:]], axis=1).astype(BF16)
            w_out = odd_w_out[j].astype(BF16)
            proj = _in_proj(h, nw[0:1], w_in, cos_t, sin_t, 0, seq).reshape(-1, b, seq, LANES)
            pad = C_PREV_CHUNKS * CHUNK
            n_groups = C_DIM // LANES
            tq = min(ATTN_C_TQ, seq)
            n_qc = tq // CHUNK
            rel_bias = _rel_bias(odd_rel_bias[j].astype(F32) * LOG2E)
            bias = jnp.stack([jnp.pad(rel_bias, ((0, 0), (0, 0), (qc * CHUNK, (n_qc - 1 - qc) * CHUNK)),
                                      constant_values=-jnp.inf) for qc in range(n_qc)], axis=1)
            bias = jnp.swapaxes(bias.reshape(n_groups, 2 * tq, tq + pad), 1, 2)
            oc = _band_attention(proj, bias, q_col=0, k_col=n_groups, v_col=2 * n_groups, n_groups=n_groups,
                                 groups_per_step=1, n_prev=C_PREV_CHUNKS, tq=tq)
            mix = [oc.reshape(-1, b * seq, LANES)]
        h = _out_ffn(mix, h, w_out, nw, ffn_w_up[layer].astype(BF16), ffn_conv_w[layer],
                     ffn_conv_b[layer][None, :], ffn_w_down[layer].astype(BF16), seq)
    return h.reshape(b, seq, d)
```

```python
import functools

import numpy as np
import jax
import jax.numpy as jnp
from jax import lax
from jax.experimental import pallas as pl
from jax.experimental.pallas import tpu as pltpu

F32 = jnp.float32
BF16 = jnp.bfloat16

D_MODEL = 1024
CHUNK = 64
HEAD_DIM = 64
ROPE_THETA = 10000.0
EPS = 1e-6
A_Q_HEADS = 8
A_KV_HEADS = 2
A_PREV_CHUNKS = 2
B_KEY_DIM = 128
B_VAL_DIM = 128
B_HEADS = 4
B_BLOCK = 32
C_HEADS = 16
C_PREV_CHUNKS = 8
REL_CLIP = 128
D_FF = 2816
CONV_WIDTH = 3
A_Q_DIM = A_Q_HEADS * HEAD_DIM
A_KV_DIM = A_KV_HEADS * HEAD_DIM
B_QK_DIM = B_HEADS * B_KEY_DIM
B_V_DIM = B_HEADS * B_VAL_DIM
EVEN_IN = A_Q_DIM + 2 * A_KV_DIM + 2 * B_QK_DIM + 2 * B_V_DIM
C_DIM = C_HEADS * HEAD_DIM
ODD_IN = 3 * C_DIM

LANES = 128
VMEM_LIMIT_BYTES = 56 * 1024 * 1024

PROJ_TILE = 1024
FFN_TILE = 512
PROJ_COLS = 256
PROJ_STREAMS = 2
FFN_COLS = 256
FFN_STREAMS = 2
HGRN_TILE = 256
HGRN_HEADS_PER_STEP = 2
ATTN_A_TQ = 128
ATTN_C_TQ = 256
ATTN_C_UNITS = 4
ONES_ROWS = 16
LOG2E = float(np.log2(np.e))
CARRY_ROWS = 8


def _rms(x, w):
    return x * lax.rsqrt(jnp.mean(x * x, axis=-1, keepdims=True) + EPS) * w


def _const_spec(shape):
    nd = len(shape)
    return pl.BlockSpec(shape, lambda *_: (0,) * nd, pipeline_mode=pl.Buffered(1))


def _in_proj_kernel(h_ref, nw_ref, w_ref, cos_ref, sin_ref, o_ref, *, rope_groups):
    n_out = w_ref.shape[1]
    ts = h_ref.shape[0] // PROJ_STREAMS
    lane = lax.broadcasted_iota(jnp.int32, (1, LANES), 1)
    first_half = (lane % HEAD_DIM) < (HEAD_DIM // 2)
    cols = list(range(0, n_out, PROJ_COLS))

    def project(s, y, c0):
        rows = slice(s * ts, (s + 1) * ts)
        acc = jnp.dot(y, w_ref[:, c0:c0 + PROJ_COLS], preferred_element_type=F32)
        for g0 in range(0, PROJ_COLS, LANES):
            part = acc[:, g0:g0 + LANES]
            if (c0 + g0) // LANES < rope_groups:
                partner = jnp.where(first_half, pltpu.roll(part, LANES - HEAD_DIM // 2, 1),
                                    pltpu.roll(part, HEAD_DIM // 2, 1))
                part = part * cos_ref[rows, :] + partner * sin_ref[rows, :]
            o_ref[(c0 + g0) // LANES, rows, :] = part.astype(BF16)

    y = [None] * PROJ_STREAMS
    for step in range(len(cols) + PROJ_STREAMS - 1):
        for s in range(PROJ_STREAMS):
            ci = step - s
            if ci == 0:
                y[s] = _rms(h_ref[s * ts:(s + 1) * ts, :], nw_ref[...]).astype(BF16)
            if 0 <= ci < len(cols):
                project(s, y[s], cols[ci])


def _in_proj(h, nw, w, cos, sin, rope_groups, seq):
    n, d = h.shape
    n_out = w.shape[1]
    n_grp = n_out // LANES
    t = min(PROJ_TILE, seq)
    tiles_per_seq = seq // t
    return pl.pallas_call(
        functools.partial(_in_proj_kernel, rope_groups=rope_groups),
        grid=(n // t,),
        in_specs=[
            pl.BlockSpec((t, d), lambda i: (i, 0)),
            _const_spec((1, d)),
            _const_spec((d, n_out)),
            pl.BlockSpec((t, LANES), lambda i: (i % tiles_per_seq, 0)),
            pl.BlockSpec((t, LANES), lambda i: (i % tiles_per_seq, 0)),
        ],
        out_specs=pl.BlockSpec((n_grp, t, LANES), lambda i: (0, i, 0)),
        out_shape=jax.ShapeDtypeStruct((n_grp, n, LANES), BF16),
        compiler_params=pltpu.CompilerParams(
            dimension_semantics=("arbitrary",), vmem_limit_bytes=VMEM_LIMIT_BYTES),
        name="in_proj",
    )(h, nw, w, cos, sin)


def _band_attention_kernel(*refs, n_units, n_groups, n_prev, tq, has_sinks):
    q_ref, k_ref, v_ref, bias_ref = refs[:4]
    sink_ref = refs[4] if has_sinks else None
    o_ref, kpad, vtpad = refs[4 + has_sinks:]

    seq = q_ref.shape[2]
    pad = n_prev * CHUNK
    tk = pad + tq
    n_tiles = seq // tq

    for u in range(n_units):
        kpad[u, 0:pad, :] = jnp.zeros((pad, LANES), BF16)
        kpad[u, pad:pad + seq, :] = k_ref[u, 0]
        vtpad[u, 0:LANES, 0:pad] = jnp.zeros((LANES, pad), BF16)
        vtpad[u, 0:LANES, pad:pad + seq] = v_ref[u, 0].astype(F32).T.astype(BF16)
        vtpad[u, LANES:LANES + ONES_ROWS, :] = jnp.ones((ONES_ROWS, pad + seq), BF16)

    lane = lax.broadcasted_iota(jnp.int32, (1, LANES), 1)
    low = lane < HEAD_DIM
    key_idx = lax.broadcasted_iota(jnp.int32, (tk, 1), 0)
    zero = jnp.zeros((), BF16)

    def scores(u, ti):
        r0 = ti * tq
        parts = []
        for j in range(n_groups):
            qj = q_ref[u * n_groups + j, 0, r0:r0 + tq, :]
            parts.append(jnp.where(low, qj, zero))
            parts.append(jnp.where(low, zero, qj))
        rows = jnp.concatenate(parts, axis=0)
        s = lax.dot_general(kpad[u, r0:r0 + tk, :], rows, (((1,), (1,)), ((), ())),
                            preferred_element_type=F32)
        s = s + bias_ref[0, u]
        if r0 < pad:
            s = jnp.where(key_idx >= pad - r0, s, -jnp.inf)
        return s

    def softmax(u, s):
        m = jnp.max(s, axis=0, keepdims=True)
        if has_sinks:
            m = jnp.maximum(m, sink_ref[0, u:u + 1])
        return jnp.exp2(s - m).astype(BF16), m

    def finish(u, ti, p, m):
        r0 = ti * tq
        o = jnp.dot(vtpad[u, :, r0:r0 + tk], p, preferred_element_type=F32)
        denom = o[LANES:LANES + 1]
        if has_sinks:
            denom = denom + jnp.exp2(sink_ref[0, u:u + 1] - m)
        o = o[0:LANES] * (1.0 / denom)
        for j in range(n_groups):
            top = o[0:HEAD_DIM, 2 * j * tq:(2 * j + 1) * tq]
            bot = o[HEAD_DIM:LANES, (2 * j + 1) * tq:(2 * j + 2) * tq]
            o_ref[u * n_groups + j, 0, r0:r0 + tq, :] = jnp.concatenate([top, bot], axis=0).T.astype(BF16)

    items = [(u, ti) for u in range(n_units) for ti in range(n_tiles)]
    s_buf, p_buf = {}, {}
    for step in range(len(items) + 2):
        if step < len(items):
            s_buf[step] = scores(*items[step])
        if 0 <= step - 1 < len(items):
            p_buf[step - 1] = softmax(items[step - 1][0], s_buf.pop(step - 1))
        if 0 <= step - 2 < len(items):
            finish(*items[step - 2], *p_buf.pop(step - 2))


def _band_mask(tq, n_prev):
    qc = np.arange(tq)[:, None] // CHUNK
    kc = np.arange(tq + n_prev * CHUNK)[None, :] // CHUNK
    return np.where((kc >= qc) & (kc <= qc + n_prev), 0.0, -np.inf).astype(np.float32)


def _band_attention(proj, bias, *, q_col, k_col, v_col, n_kv, groups_per_kv, units_per_step, n_prev, tq,
                    sinks=None):
    _, b, seq, _ = proj.shape
    n_steps = n_kv // units_per_step
    q_blk = units_per_step * groups_per_kv
    tk = tq + n_prev * CHUNK
    rows = groups_per_kv * 2 * tq
    in_specs = [
        pl.BlockSpec((q_blk, 1, seq, LANES), lambda j, i: (q_col // q_blk + j, i, 0, 0)),
        pl.BlockSpec((units_per_step, 1, seq, LANES), lambda j, i: (k_col // units_per_step + j, i, 0, 0)),
        pl.BlockSpec((units_per_step, 1, seq, LANES), lambda j, i: (v_col // units_per_step + j, i, 0, 0)),
        pl.BlockSpec((1, units_per_step, tk, rows), lambda j, i: (j, 0, 0, 0)),
    ]
    args = [proj, proj, proj, bias]
    if sinks is not None:
        in_specs.append(pl.BlockSpec((1, units_per_step, rows), lambda j, i: (j, 0, 0)))
        args.append(sinks)
    return pl.pallas_call(
        functools.partial(_band_attention_kernel, n_units=units_per_step, n_groups=groups_per_kv,
                          n_prev=n_prev, tq=tq, has_sinks=sinks is not None),
        grid=(n_steps, b),
        in_specs=in_specs,
        out_specs=pl.BlockSpec((q_blk, 1, seq, LANES), lambda j, i: (j, i, 0, 0)),
        out_shape=jax.ShapeDtypeStruct((n_kv * groups_per_kv, b, seq, LANES), BF16),
        scratch_shapes=[pltpu.VMEM((units_per_step, seq + n_prev * CHUNK, LANES), BF16),
                        pltpu.VMEM((units_per_step, LANES + ONES_ROWS, seq + n_prev * CHUNK), BF16)],
        compiler_params=pltpu.CompilerParams(
            dimension_semantics=("arbitrary", "arbitrary"), vmem_limit_bytes=VMEM_LIMIT_BYTES),
        name="band_attention",
    )(*args)


def _hgrn_kernel(q_ref, f_ref, i_ref, g_ref, lbl_ref, nw_ref, o_ref, *, layer_j):
    n_heads, seq, _ = q_ref.shape
    t = min(HGRN_TILE, seq)
    n_tiles = seq // t
    n_blk = t // B_BLOCK

    logits = lbl_ref[...]
    e = jnp.exp(logits - jnp.max(logits, axis=0, keepdims=True))
    sm = e / jnp.sum(e, axis=0, keepdims=True)
    lb = sm[0:1]
    for r in range(1, layer_j + 1):
        lb = lb + sm[r:r + 1]

    row = lax.broadcasted_iota(jnp.int32, (t, t), 0)
    col = lax.broadcasted_iota(jnp.int32, (t, t), 1)
    same_blk = (row // B_BLOCK) == (col // B_BLOCK)
    tril = jnp.logical_and(same_blk, col <= row)
    cum_mat = jnp.where(tril, 1.0, 0.0).astype(BF16)
    nw = nw_ref[...]

    def block_diag(x):
        cols = []
        for n in range(n_blk):
            pieces = [x[n * B_BLOCK:(n + 1) * B_BLOCK]]
            if n > 0:
                pieces.insert(0, jnp.zeros((n * B_BLOCK, x.shape[1]), x.dtype))
            if n < n_blk - 1:
                pieces.append(jnp.zeros(((n_blk - 1 - n) * B_BLOCK, x.shape[1]), x.dtype))
            cols.append(jnp.concatenate(pieces, axis=0))
        return jnp.concatenate(cols, axis=1)

    def state_free_part(hd, ti):
        r0 = ti * t
        lb_h = lb[:, hd * B_KEY_DIM:(hd + 1) * B_KEY_DIM]
        fl = f_ref[hd, r0:r0 + t, :].astype(F32)
        f = lb_h + (1.0 - lb_h) * jax.nn.sigmoid(fl)
        log_f = jnp.log(f)
        hi = log_f.astype(BF16)
        r1 = log_f - hi.astype(F32)
        mid = r1.astype(BF16)
        lo = (r1 - mid.astype(F32)).astype(BF16)
        parts = jnp.dot(cum_mat, jnp.concatenate([hi, mid, lo], axis=1), preferred_element_type=F32)
        cum = (parts[:, 0:LANES] + parts[:, LANES:2 * LANES]) + parts[:, 2 * LANES:3 * LANES]
        last = jnp.broadcast_to(cum.reshape(n_blk, B_BLOCK, B_KEY_DIM)[:, B_BLOCK - 1:B_BLOCK, :],
                                (n_blk, B_BLOCK, B_KEY_DIM)).reshape(t, B_KEY_DIM)
        k = 1.0 - f
        q = q_ref[hd, r0:r0 + t, :].astype(F32)
        q_dec = (q * jnp.exp(cum)).astype(BF16)
        k_dec = (k * jnp.exp(-cum)).astype(BF16)
        k_end = (k * jnp.exp(last - cum)).astype(BF16)
        blk_decay = jnp.exp(last)
        v = i_ref[hd, r0:r0 + t, :]
        s = lax.dot_general(q_dec, k_dec, (((1,), (1,)), ((), ())), preferred_element_type=F32)
        s = jnp.where(tril, s, 0.0)
        o_intra = jnp.dot(s.astype(BF16), v, preferred_element_type=F32)
        upd = lax.dot_general(v, block_diag(k_end), (((0,), (0,)), ((), ())), preferred_element_type=F32)
        return q_dec, blk_decay, upd, o_intra

    def state_part(hd, ti, state_t, q_dec, blk_decay, upd, o_intra):
        r0 = ti * t
        o_inter = []
        for n in range(n_blk):
            o_inter.append(lax.dot_general(q_dec[n * B_BLOCK:(n + 1) * B_BLOCK], state_t.astype(BF16),
                                           (((1,), (1,)), ((), ())), preferred_element_type=F32))
            state_t = (blk_decay[n * B_BLOCK:n * B_BLOCK + 1] * state_t
                       + upd[:, n * B_KEY_DIM:(n + 1) * B_KEY_DIM])
        o = o_intra + jnp.concatenate(o_inter, axis=0)
        o = _rms(o, nw[:, hd * B_VAL_DIM:(hd + 1) * B_VAL_DIM])
        g = g_ref[hd, r0:r0 + t, :].astype(F32)
        o_ref[hd, r0:r0 + t, :] = (o * (g * jax.nn.sigmoid(g))).astype(BF16)
        return state_t

    items = [(hd, ti) for hd in range(n_heads) for ti in range(n_tiles)]
    nxt = state_free_part(*items[0])
    for n, (hd, ti) in enumerate(items):
        cur = nxt
        if n + 1 < len(items):
            nxt = state_free_part(*items[n + 1])
        if ti == 0:
            state_t = jnp.zeros((B_VAL_DIM, B_KEY_DIM), F32)
        state_t = state_part(hd, ti, state_t, *cur)


def _hgrn(proj, lb_logits, norm_w, *, q_col, f_col, i_col, g_col, layer_j):
    _, b, seq, _ = proj.shape
    n_layers = lb_logits.shape[0]

    hps = HGRN_HEADS_PER_STEP

    def col_spec(col):
        return pl.BlockSpec((hps, None, seq, LANES), lambda i, h: (col // hps + h, i, 0, 0))

    return pl.pallas_call(
        functools.partial(_hgrn_kernel, layer_j=layer_j),
        grid=(b, B_HEADS // hps),
        in_specs=[col_spec(q_col), col_spec(f_col), col_spec(i_col), col_spec(g_col),
                  pl.BlockSpec((n_layers, hps * LANES), lambda i, h: (0, h)),
                  pl.BlockSpec((1, hps * LANES), lambda i, h: (0, h))],
        out_specs=pl.BlockSpec((hps, None, seq, LANES), lambda i, h: (h, i, 0, 0)),
        out_shape=jax.ShapeDtypeStruct((B_HEADS, b, seq, LANES), BF16),
        compiler_params=pltpu.CompilerParams(
            dimension_semantics=("arbitrary", "arbitrary"), vmem_limit_bytes=VMEM_LIMIT_BYTES),
        name="hgrn2",
    )(proj, proj, proj, proj, lb_logits, norm_w)


def _out_ffn_kernel(*refs, n_mix, tiles_per_seq):
    mix_refs = refs[:n_mix]
    (h_ref, wout_ref, nw_ref, wup_ref, cw_ref, cb_ref, wdn_ref,
     hout_ref, g_scr, ubuf, carry) = refs[n_mix:]
    t = h_ref.shape[0]

    @pl.when(pl.program_id(0) % tiles_per_seq == 0)
    def _():
        carry[...] = jnp.zeros(carry.shape, F32)

    ts = t // FFN_STREAMS
    sqrt_half = np.float32(np.sqrt(0.5))
    cols = list(range(0, D_FF, FFN_COLS))

    def mix_dot(s):
        mix = jnp.concatenate([r[g, s * ts:(s + 1) * ts, :] for r in mix_refs for g in range(r.shape[0])], axis=1)
        return jnp.dot(mix, wout_ref[...], preferred_element_type=F32)

    def mix_norm(s, mixed):
        h_mid = h_ref[s * ts:(s + 1) * ts, :] + _rms(mixed, nw_ref[1:2])
        return h_mid, _rms(h_mid, nw_ref[2:3]).astype(BF16)

    def ffn_chunk(s, y, c0):
        u = jnp.dot(y, wup_ref[:, c0:c0 + FFN_COLS], preferred_element_type=F32)
        v = jnp.dot(y, wup_ref[:, D_FF + c0:D_FF + c0 + FFN_COLS], preferred_element_type=F32)
        ubuf[s, 0:CARRY_ROWS, :] = carry[:, c0:c0 + FFN_COLS]
        ubuf[s, CARRY_ROWS:CARRY_ROWS + ts, :] = u
        carry[:, c0:c0 + FFN_COLS] = u[ts - CARRY_ROWS:ts]
        u1 = ubuf[s, CARRY_ROWS - 1:CARRY_ROWS - 1 + ts, :]
        u2 = ubuf[s, CARRY_ROWS - 2:CARRY_ROWS - 2 + ts, :]
        c = cb_ref[:, c0:c0 + FFN_COLS] + cw_ref[0:1, c0:c0 + FFN_COLS] * u2
        c = c + cw_ref[1:2, c0:c0 + FFN_COLS] * u1
        c = c + cw_ref[2:3, c0:c0 + FFN_COLS] * u
        gelu = 0.5 * c * (1.0 + lax.erf(c * sqrt_half))
        g_scr[s * ts:(s + 1) * ts, c0:c0 + FFN_COLS] = (gelu * v).astype(BF16)

    def down_dot(s):
        return jnp.dot(g_scr[s * ts:(s + 1) * ts, :], wdn_ref[...], preferred_element_type=F32)

    mixed = [mix_dot(s) for s in range(FFN_STREAMS)]
    h_mid, y = [None] * FFN_STREAMS, [None] * FFN_STREAMS
    for step in range(len(cols) + FFN_STREAMS - 1):
        for s in range(FFN_STREAMS):
            ci = step - s
            if ci == 0:
                h_mid[s], y[s] = mix_norm(s, mixed[s])
            if 0 <= ci < len(cols):
                ffn_chunk(s, y[s], cols[ci])
    down = [down_dot(s) for s in range(FFN_STREAMS)]
    for s in range(FFN_STREAMS):
        hout_ref[s * ts:(s + 1) * ts, :] = h_mid[s] + _rms(down[s], nw_ref[3:4])


def _layer_spec(stacked, layer):
    return pl.BlockSpec((None,) + stacked.shape[1:], lambda *_: (layer, 0, 0), pipeline_mode=pl.Buffered(1))


def _out_ffn(mix_list, h, w_out, nw, w_up, conv_w, conv_b, w_down, layer, seq):
    n, d = h.shape
    t = min(FFN_TILE, seq)
    tiles_per_seq = seq // t
    in_specs = [pl.BlockSpec((m.shape[0], t, LANES), lambda i: (0, i, 0)) for m in mix_list]
    in_specs += [
        pl.BlockSpec((t, d), lambda i: (i, 0)),
        _const_spec(w_out.shape),
        _layer_spec(nw, layer),
        _layer_spec(w_up, layer),
        _layer_spec(conv_w, layer),
        _layer_spec(conv_b, layer),
        _layer_spec(w_down, layer),
    ]
    return pl.pallas_call(
        functools.partial(_out_ffn_kernel, n_mix=len(mix_list), tiles_per_seq=tiles_per_seq),
        grid=(n // t,),
        in_specs=in_specs,
        out_specs=pl.BlockSpec((t, d), lambda i: (i, 0)),
        out_shape=jax.ShapeDtypeStruct((n, d), F32),
        scratch_shapes=[pltpu.VMEM((t, D_FF), BF16),
                        pltpu.VMEM((FFN_STREAMS, t // FFN_STREAMS + CARRY_ROWS, FFN_COLS), F32),
                        pltpu.VMEM((CARRY_ROWS, D_FF), F32)],
        compiler_params=pltpu.CompilerParams(
            dimension_semantics=("arbitrary",), vmem_limit_bytes=VMEM_LIMIT_BYTES),
        name="out_ffn",
    )(*mix_list, h, w_out, nw, w_up, conv_w, conv_b, w_down)


def _rope_tables(seq):
    inv_freq = ROPE_THETA ** (-jnp.arange(0, HEAD_DIM, 2, dtype=F32) / HEAD_DIM)
    ang = jnp.arange(seq, dtype=F32)[:, None] * inv_freq[None, :]
    cos, sin = jnp.cos(ang), jnp.sin(ang)
    reps = LANES // HEAD_DIM
    cos_t = jnp.tile(jnp.concatenate([cos, cos], axis=-1), (1, reps))
    sin_t = jnp.tile(jnp.concatenate([-sin, sin], axis=-1), (1, reps))
    return cos_t, sin_t


_A_HEAD_ORDER = tuple(h for j in range(A_Q_HEADS // 2) for h in (j, j + A_Q_HEADS // 2))


def _rel_bias(table):
    pad = C_PREV_CHUNKS * CHUNK
    band = pad + CHUNK
    period = band + CHUNK
    w = np.arange(period)
    diff = np.where(w < band, w, w - period)
    idx = np.clip(pad - diff, -REL_CLIP, REL_CLIP) + REL_CLIP
    vals = table[:, idx]
    skew = jnp.tile(vals, (1, CHUNK))[:, :CHUNK * (period - 1)].reshape(-1, CHUNK, period - 1)
    return skew[:, :, :band]


def _head_cols(order):
    return np.concatenate([np.arange(h * HEAD_DIM, (h + 1) * HEAD_DIM) for h in order])


def kernel(x, even_w_in, even_w_out, even_sinks, hgrn_lb_logits, hgrn_norm_w, odd_w_in, odd_w_out,
           odd_rel_bias, ffn_w_up, ffn_conv_w, ffn_conv_b, ffn_w_down, norm_w):
    b, seq, d = x.shape
    depth = norm_w.shape[0]
    scale = HEAD_DIM ** -0.5 * LOG2E
    cos_t, sin_t = _rope_tables(seq)
    a_cols = _head_cols(_A_HEAD_ORDER)
    w_up_all = ffn_w_up.astype(BF16)
    w_down_all = ffn_w_down.astype(BF16)
    conv_b_all = ffn_conv_b[:, None, :]

    h = x.reshape(b * seq, d)
    for layer in range(depth):
        nw = norm_w[layer]
        j = layer // 2
        if layer % 2 == 0:
            w_in = even_w_in[j]
            w_in = jnp.concatenate([w_in[:, :A_Q_DIM][:, a_cols] * scale, w_in[:, A_Q_DIM:]], axis=1).astype(BF16)
            w_out = jnp.concatenate([even_w_out[j][:A_Q_DIM][a_cols], even_w_out[j][A_Q_DIM:]], axis=0).astype(BF16)
            rope_groups = (A_Q_DIM + A_KV_DIM) // LANES
            proj = _in_proj(h, nw[0:1], w_in, cos_t, sin_t, rope_groups, seq).reshape(-1, b, seq, LANES)
            tq = min(ATTN_A_TQ, seq)
            sinks = jnp.repeat(even_sinks[j][np.array(_A_HEAD_ORDER)].astype(F32) * LOG2E, tq)[None, None, :]
            n_groups = A_Q_DIM // LANES
            mask = jnp.asarray(np.tile(_band_mask(tq, A_PREV_CHUNKS).T, (1, 2 * n_groups))[None, None])
            oa = _band_attention(proj, mask, q_col=0, k_col=n_groups, v_col=n_groups + 1, n_kv=1,
                                 groups_per_kv=n_groups, units_per_step=1, n_prev=A_PREV_CHUNKS, tq=tq,
                                 sinks=sinks)
            c0 = (A_Q_DIM + 2 * A_KV_DIM) // LANES
            ob = _hgrn(proj, hgrn_lb_logits, hgrn_norm_w[j:j + 1],
                       q_col=c0, f_col=c0 + B_HEADS, i_col=c0 + 2 * B_HEADS, g_col=c0 + 3 * B_HEADS, layer_j=j)
            mix = [oa.reshape(-1, b * seq, LANES), ob.reshape(-1, b * seq, LANES)]
        else:
            w_in = odd_w_in[j]
            w_in = jnp.concatenate([w_in[:, :C_DIM] * scale, w_in[:, C_DIM:]], axis=1).astype(BF16)
            w_out = odd_w_out[j].astype(BF16)
            proj = _in_proj(h, nw[0:1], w_in, cos_t, sin_t, 0, seq).reshape(-1, b, seq, LANES)
            pad = C_PREV_CHUNKS * CHUNK
            n_groups = C_DIM // LANES
            tq = min(ATTN_C_TQ, seq)
            n_qc = tq // CHUNK
            rel_bias = _rel_bias(odd_rel_bias[j].astype(F32) * LOG2E)
            bias = jnp.stack([jnp.pad(rel_bias, ((0, 0), (0, 0), (qc * CHUNK, (n_qc - 1 - qc) * CHUNK)),
                                      constant_values=-jnp.inf) for qc in range(n_qc)], axis=1)
            bias = jnp.swapaxes(bias.reshape(n_groups, 2 * tq, tq + pad), 1, 2)
            bias = bias.reshape(n_groups // ATTN_C_UNITS, ATTN_C_UNITS, tq + pad, 2 * tq)
            oc = _band_attention(proj, bias, q_col=0, k_col=n_groups, v_col=2 * n_groups, n_kv=n_groups,
                                 groups_per_kv=1, units_per_step=ATTN_C_UNITS, n_prev=C_PREV_CHUNKS, tq=tq)
            mix = [oc.reshape(-1, b * seq, LANES)]
        h = _out_ffn(mix, h, w_out, norm_w, w_up_all, ffn_conv_w, conv_b_all, w_down_all, layer, seq)
    return h.reshape(b, seq, d)
```

```python
import functools

import numpy as np
import jax
import jax.numpy as jnp
from jax import lax
from jax.experimental import pallas as pl
from jax.experimental.pallas import tpu as pltpu

F32 = jnp.float32
BF16 = jnp.bfloat16

D_MODEL = 1024
CHUNK = 64
HEAD_DIM = 64
ROPE_THETA = 10000.0
EPS = 1e-6
A_Q_HEADS = 8
A_KV_HEADS = 2
A_PREV_CHUNKS = 2
B_KEY_DIM = 128
B_VAL_DIM = 128
B_HEADS = 4
B_BLOCK = 32
C_HEADS = 16
C_PREV_CHUNKS = 8
REL_CLIP = 128
D_FF = 2816
CONV_WIDTH = 3
A_Q_DIM = A_Q_HEADS * HEAD_DIM
A_KV_DIM = A_KV_HEADS * HEAD_DIM
B_QK_DIM = B_HEADS * B_KEY_DIM
B_V_DIM = B_HEADS * B_VAL_DIM
EVEN_IN = A_Q_DIM + 2 * A_KV_DIM + 2 * B_QK_DIM + 2 * B_V_DIM
C_DIM = C_HEADS * HEAD_DIM
ODD_IN = 3 * C_DIM

LANES = 128
VMEM_LIMIT_BYTES = 56 * 1024 * 1024

PROJ_TILE = 1024
FFN_TILE = 512
PROJ_COLS = 256
PROJ_STREAMS = 2
FFN_COLS = 256
FFN_STREAMS = 2
HGRN_TILE = 256
HGRN_HEADS_PER_STEP = 2
ATTN_A_TQ = 128
ATTN_C_TQ = 256
ATTN_C_UNITS = 4
ONES_ROWS = 16
LOG2E = float(np.log2(np.e))
CARRY_ROWS = 8


def _rms(x, w):
    return x * lax.rsqrt(jnp.mean(x * x, axis=-1, keepdims=True) + EPS) * w


def _const_spec(shape):
    nd = len(shape)
    return pl.BlockSpec(shape, lambda *_: (0,) * nd, pipeline_mode=pl.Buffered(1))


def _in_proj_kernel(h_ref, nw_ref, w_ref, cos_ref, sin_ref, o_ref, *, rope_groups):
    n_out = w_ref.shape[1]
    ts = h_ref.shape[0] // PROJ_STREAMS
    lane = lax.broadcasted_iota(jnp.int32, (1, LANES), 1)
    first_half = (lane % HEAD_DIM) < (HEAD_DIM // 2)
    cols = list(range(0, n_out, PROJ_COLS))

    def project(s, y, c0):
        rows = slice(s * ts, (s + 1) * ts)
        acc = jnp.dot(y, w_ref[:, c0:c0 + PROJ_COLS], preferred_element_type=F32)
        for g0 in range(0, PROJ_COLS, LANES):
            part = acc[:, g0:g0 + LANES]
            if (c0 + g0) // LANES < rope_groups:
                partner = jnp.where(first_half, pltpu.roll(part, LANES - HEAD_DIM // 2, 1),
                                    pltpu.roll(part, HEAD_DIM // 2, 1))
                part = part * cos_ref[rows, :] + partner * sin_ref[rows, :]
            o_ref[(c0 + g0) // LANES, rows, :] = part.astype(BF16)

    y = [None] * PROJ_STREAMS
    for step in range(len(cols) + PROJ_STREAMS - 1):
        for s in range(PROJ_STREAMS):
            ci = step - s
            if ci == 0:
                y[s] = _rms(h_ref[s * ts:(s + 1) * ts, :], nw_ref[...]).astype(BF16)
            if 0 <= ci < len(cols):
                project(s, y[s], cols[ci])


def _in_proj(h, nw, w, cos, sin, rope_groups, seq):
    n, d = h.shape
    n_out = w.shape[1]
    n_grp = n_out // LANES
    t = min(PROJ_TILE, seq)
    tiles_per_seq = seq // t
    return pl.pallas_call(
        functools.partial(_in_proj_kernel, rope_groups=rope_groups),
        grid=(n // t,),
        in_specs=[
            pl.BlockSpec((t, d), lambda i: (i, 0)),
            _const_spec((1, d)),
            _const_spec((d, n_out)),
            pl.BlockSpec((t, LANES), lambda i: (i % tiles_per_seq, 0)),
            pl.BlockSpec((t, LANES), lambda i: (i % tiles_per_seq, 0)),
        ],
        out_specs=pl.BlockSpec((n_grp, t, LANES), lambda i: (0, i, 0)),
        out_shape=jax.ShapeDtypeStruct((n_grp, n, LANES), BF16),
        compiler_params=pltpu.CompilerParams(
            dimension_semantics=("arbitrary",), vmem_limit_bytes=VMEM_LIMIT_BYTES),
        name="in_proj",
    )(h, nw, w, cos, sin)


def _band_attention_kernel(*refs, n_units, n_groups, n_prev, tq, has_sinks):
    q_ref, k_ref, v_ref, bias_ref = refs[:4]
    sink_ref = refs[4] if has_sinks else None
    o_ref, kpad, vtpad = refs[4 + has_sinks:]

    seq = q_ref.shape[2]
    pad = n_prev * CHUNK
    tk = pad + tq
    n_tiles = seq // tq

    row_major_pv = not has_sinks
    for u in range(n_units):
        kpad[u, 0:pad, :] = jnp.zeros((pad, LANES), BF16)
        kpad[u, pad:pad + seq, :] = k_ref[u, 0]
        if row_major_pv:
            vtpad[u, 0:pad, 0:LANES] = jnp.zeros((pad, LANES), BF16)
            vtpad[u, pad:pad + seq, 0:LANES] = v_ref[u, 0]
            vtpad[u, :, LANES:2 * LANES] = jnp.ones((pad + seq, LANES), BF16)
        else:
            vtpad[u, 0:LANES, 0:pad] = jnp.zeros((LANES, pad), BF16)
            vtpad[u, 0:LANES, pad:pad + seq] = v_ref[u, 0].astype(F32).T.astype(BF16)
            vtpad[u, LANES:LANES + ONES_ROWS, :] = jnp.ones((ONES_ROWS, pad + seq), BF16)

    lane = lax.broadcasted_iota(jnp.int32, (1, LANES), 1)
    low = lane < HEAD_DIM
    key_idx = lax.broadcasted_iota(jnp.int32, (tk, 1), 0)
    zero = jnp.zeros((), BF16)

    def scores(u, ti):
        r0 = ti * tq
        parts = []
        for j in range(n_groups):
            qj = q_ref[u * n_groups + j, 0, r0:r0 + tq, :]
            parts.append(jnp.where(low, qj, zero))
            parts.append(jnp.where(low, zero, qj))
        rows = jnp.concatenate(parts, axis=0)
        s = lax.dot_general(kpad[u, r0:r0 + tk, :], rows, (((1,), (1,)), ((), ())),
                            preferred_element_type=F32)
        s = s + bias_ref[0, u]
        if r0 < pad:
            s = jnp.where(key_idx >= pad - r0, s, -jnp.inf)
        return s

    def softmax(u, s):
        m = jnp.max(s, axis=0, keepdims=True)
        if has_sinks:
            m = jnp.maximum(m, sink_ref[0, u:u + 1])
        return jnp.exp2(s - m).astype(BF16), m

    def finish(u, ti, p, m):
        r0 = ti * tq
        if row_major_pv:
            o = lax.dot_general(p, vtpad[u, r0:r0 + tk, :], (((0,), (0,)), ((), ())),
                                preferred_element_type=F32)
            o = o[:, 0:LANES] * (1.0 / o[:, LANES:2 * LANES])
            for j in range(n_groups):
                top = o[2 * j * tq:(2 * j + 1) * tq]
                bot = o[(2 * j + 1) * tq:(2 * j + 2) * tq]
                o_ref[u * n_groups + j, 0, r0:r0 + tq, :] = jnp.where(low, top, bot).astype(BF16)
            return
        o = jnp.dot(vtpad[u, :, r0:r0 + tk], p, preferred_element_type=F32)
        denom = o[LANES:LANES + 1]
        if has_sinks:
            denom = denom + jnp.exp2(sink_ref[0, u:u + 1] - m)
        o = o[0:LANES] * (1.0 / denom)
        for j in range(n_groups):
            top = o[0:HEAD_DIM, 2 * j * tq:(2 * j + 1) * tq]
            bot = o[HEAD_DIM:LANES, (2 * j + 1) * tq:(2 * j + 2) * tq]
            o_ref[u * n_groups + j, 0, r0:r0 + tq, :] = jnp.concatenate([top, bot], axis=0).T.astype(BF16)

    items = [(u, ti) for u in range(n_units) for ti in range(n_tiles)]
    s_buf, p_buf = {}, {}
    for step in range(len(items) + 2):
        if step < len(items):
            s_buf[step] = scores(*items[step])
        if 0 <= step - 1 < len(items):
            p_buf[step - 1] = softmax(items[step - 1][0], s_buf.pop(step - 1))
        if 0 <= step - 2 < len(items):
            finish(*items[step - 2], *p_buf.pop(step - 2))


def _band_mask(tq, n_prev):
    qc = np.arange(tq)[:, None] // CHUNK
    kc = np.arange(tq + n_prev * CHUNK)[None, :] // CHUNK
    return np.where((kc >= qc) & (kc <= qc + n_prev), 0.0, -np.inf).astype(np.float32)


def _band_attention(proj, bias, *, q_col, k_col, v_col, n_kv, groups_per_kv, units_per_step, n_prev, tq,
                    sinks=None):
    _, b, seq, _ = proj.shape
    n_steps = n_kv // units_per_step
    q_blk = units_per_step * groups_per_kv
    tk = tq + n_prev * CHUNK
    rows = groups_per_kv * 2 * tq
    in_specs = [
        pl.BlockSpec((q_blk, 1, seq, LANES), lambda j, i: (q_col // q_blk + j, i, 0, 0)),
        pl.BlockSpec((units_per_step, 1, seq, LANES), lambda j, i: (k_col // units_per_step + j, i, 0, 0)),
        pl.BlockSpec((units_per_step, 1, seq, LANES), lambda j, i: (v_col // units_per_step + j, i, 0, 0)),
        pl.BlockSpec((1, units_per_step, tk, rows), lambda j, i: (j, 0, 0, 0)),
    ]
    args = [proj, proj, proj, bias]
    if sinks is not None:
        in_specs.append(pl.BlockSpec((1, units_per_step, rows), lambda j, i: (j, 0, 0)))
        args.append(sinks)
    return pl.pallas_call(
        functools.partial(_band_attention_kernel, n_units=units_per_step, n_groups=groups_per_kv,
                          n_prev=n_prev, tq=tq, has_sinks=sinks is not None),
        grid=(n_steps, b),
        in_specs=in_specs,
        out_specs=pl.BlockSpec((q_blk, 1, seq, LANES), lambda j, i: (j, i, 0, 0)),
        out_shape=jax.ShapeDtypeStruct((n_kv * groups_per_kv, b, seq, LANES), BF16),
        scratch_shapes=[pltpu.VMEM((units_per_step, seq + n_prev * CHUNK, LANES), BF16),
                        pltpu.VMEM((units_per_step, seq + n_prev * CHUNK, 2 * LANES) if sinks is None else
                                   (units_per_step, LANES + ONES_ROWS, seq + n_prev * CHUNK), BF16)],
        compiler_params=pltpu.CompilerParams(
            dimension_semantics=("arbitrary", "arbitrary"), vmem_limit_bytes=VMEM_LIMIT_BYTES),
        name="band_attention",
    )(*args)


def _hgrn_kernel(q_ref, f_ref, i_ref, g_ref, lbl_ref, nw_ref, o_ref, *, layer_j):
    n_heads, seq, _ = q_ref.shape
    t = min(HGRN_TILE, seq)
    n_tiles = seq // t
    n_blk = t // B_BLOCK

    logits = lbl_ref[...]
    e = jnp.exp(logits - jnp.max(logits, axis=0, keepdims=True))
    sm = e / jnp.sum(e, axis=0, keepdims=True)
    lb = sm[0:1]
    for r in range(1, layer_j + 1):
        lb = lb + sm[r:r + 1]

    row = lax.broadcasted_iota(jnp.int32, (t, t), 0)
    col = lax.broadcasted_iota(jnp.int32, (t, t), 1)
    same_blk = (row // B_BLOCK) == (col // B_BLOCK)
    tril = jnp.logical_and(same_blk, col <= row)
    cum_mat = jnp.where(tril, 1.0, 0.0).astype(BF16)
    nw = nw_ref[...]

    def block_diag(x):
        cols = []
        for n in range(n_blk):
            pieces = [x[n * B_BLOCK:(n + 1) * B_BLOCK]]
            if n > 0:
                pieces.insert(0, jnp.zeros((n * B_BLOCK, x.shape[1]), x.dtype))
            if n < n_blk - 1:
                pieces.append(jnp.zeros(((n_blk - 1 - n) * B_BLOCK, x.shape[1]), x.dtype))
            cols.append(jnp.concatenate(pieces, axis=0))
        return jnp.concatenate(cols, axis=1)

    def state_free_part(hd, ti):
        r0 = ti * t
        lb_h = lb[:, hd * B_KEY_DIM:(hd + 1) * B_KEY_DIM]
        fl = f_ref[hd, r0:r0 + t, :].astype(F32)
        f = lb_h + (1.0 - lb_h) * jax.nn.sigmoid(fl)
        log_f = jnp.log(f)
        hi = log_f.astype(BF16)
        r1 = log_f - hi.astype(F32)
        mid = r1.astype(BF16)
        lo = (r1 - mid.astype(F32)).astype(BF16)
        parts = jnp.dot(cum_mat, jnp.concatenate([hi, mid, lo], axis=1), preferred_element_type=F32)
        cum = (parts[:, 0:LANES] + parts[:, LANES:2 * LANES]) + parts[:, 2 * LANES:3 * LANES]
        last = jnp.broadcast_to(cum.reshape(n_blk, B_BLOCK, B_KEY_DIM)[:, B_BLOCK - 1:B_BLOCK, :],
                                (n_blk, B_BLOCK, B_KEY_DIM)).reshape(t, B_KEY_DIM)
        k = 1.0 - f
        q = q_ref[hd, r0:r0 + t, :].astype(F32)
        q_dec = (q * jnp.exp(cum)).astype(BF16)
        k_dec = (k * jnp.exp(-cum)).astype(BF16)
        k_end = (k * jnp.exp(last - cum)).astype(BF16)
        blk_decay = jnp.exp(last)
        v = i_ref[hd, r0:r0 + t, :]
        s = lax.dot_general(q_dec, k_dec, (((1,), (1,)), ((), ())), preferred_element_type=F32)
        s = jnp.where(tril, s, 0.0)
        o_intra = jnp.dot(s.astype(BF16), v, preferred_element_type=F32)
        upd = lax.dot_general(v, block_diag(k_end), (((0,), (0,)), ((), ())), preferred_element_type=F32)
        return q_dec, blk_decay, upd, o_intra

    def state_part(hd, ti, state_t, q_dec, blk_decay, upd, o_intra):
        r0 = ti * t
        o_inter = []
        for n in range(n_blk):
            o_inter.append(lax.dot_general(q_dec[n * B_BLOCK:(n + 1) * B_BLOCK], state_t.astype(BF16),
                                           (((1,), (1,)), ((), ())), preferred_element_type=F32))
            state_t = (blk_decay[n * B_BLOCK:n * B_BLOCK + 1] * state_t
                       + upd[:, n * B_KEY_DIM:(n + 1) * B_KEY_DIM])
        o = o_intra + jnp.concatenate(o_inter, axis=0)
        o = _rms(o, nw[:, hd * B_VAL_DIM:(hd + 1) * B_VAL_DIM])
        g = g_ref[hd, r0:r0 + t, :].astype(F32)
        o_ref[hd, r0:r0 + t, :] = (o * (g * jax.nn.sigmoid(g))).astype(BF16)
        return state_t

    items = [(hd, ti) for hd in range(n_heads) for ti in range(n_tiles)]
    nxt = state_free_part(*items[0])
    for n, (hd, ti) in enumerate(items):
        cur = nxt
        if n + 1 < len(items):
            nxt = state_free_part(*items[n + 1])
        if ti == 0:
            state_t = jnp.zeros((B_VAL_DIM, B_KEY_DIM), F32)
        state_t = state_part(hd, ti, state_t, *cur)


def _hgrn(proj, lb_logits, norm_w, *, q_col, f_col, i_col, g_col, layer_j):
    _, b, seq, _ = proj.shape
    n_layers = lb_logits.shape[0]

    hps = HGRN_HEADS_PER_STEP

    def col_spec(col):
        return pl.BlockSpec((hps, None, seq, LANES), lambda i, h: (col // hps + h, i, 0, 0))

    return pl.pallas_call(
        functools.partial(_hgrn_kernel, layer_j=layer_j),
        grid=(b, B_HEADS // hps),
        in_specs=[col_spec(q_col), col_spec(f_col), col_spec(i_col), col_spec(g_col),
                  pl.BlockSpec((n_layers, hps * LANES), lambda i, h: (0, h)),
                  pl.BlockSpec((1, hps * LANES), lambda i, h: (0, h))],
        out_specs=pl.BlockSpec((hps, None, seq, LANES), lambda i, h: (h, i, 0, 0)),
        out_shape=jax.ShapeDtypeStruct((B_HEADS, b, seq, LANES), BF16),
        compiler_params=pltpu.CompilerParams(
            dimension_semantics=("arbitrary", "arbitrary"), vmem_limit_bytes=VMEM_LIMIT_BYTES),
        name="hgrn2",
    )(proj, proj, proj, proj, lb_logits, norm_w)


def _out_ffn_kernel(*refs, n_mix, tiles_per_seq):
    mix_refs = refs[:n_mix]
    (h_ref, wout_ref, nw_ref, wup_ref, cw_ref, cb_ref, wdn_ref,
     hout_ref, g_scr, ubuf, carry) = refs[n_mix:]
    t = h_ref.shape[0]

    @pl.when(pl.program_id(0) % tiles_per_seq == 0)
    def _():
        carry[...] = jnp.zeros(carry.shape, F32)

    ts = t // FFN_STREAMS
    sqrt_half = np.float32(np.sqrt(0.5))
    cols = list(range(0, D_FF, FFN_COLS))

    def mix_dot(s):
        mix = jnp.concatenate([r[g, s * ts:(s + 1) * ts, :] for r in mix_refs for g in range(r.shape[0])], axis=1)
        return jnp.dot(mix, wout_ref[...], preferred_element_type=F32)

    def mix_norm(s, mixed):
        h_mid = h_ref[s * ts:(s + 1) * ts, :] + _rms(mixed, nw_ref[1:2])
        return h_mid, _rms(h_mid, nw_ref[2:3]).astype(BF16)

    def ffn_chunk(s, y, c0):
        u = jnp.dot(y, wup_ref[:, c0:c0 + FFN_COLS], preferred_element_type=F32)
        v = jnp.dot(y, wup_ref[:, D_FF + c0:D_FF + c0 + FFN_COLS], preferred_element_type=F32)
        ubuf[s, 0:CARRY_ROWS, :] = carry[:, c0:c0 + FFN_COLS]
        ubuf[s, CARRY_ROWS:CARRY_ROWS + ts, :] = u
        carry[:, c0:c0 + FFN_COLS] = u[ts - CARRY_ROWS:ts]
        u1 = ubuf[s, CARRY_ROWS - 1:CARRY_ROWS - 1 + ts, :]
        u2 = ubuf[s, CARRY_ROWS - 2:CARRY_ROWS - 2 + ts, :]
        c = cb_ref[:, c0:c0 + FFN_COLS] + cw_ref[0:1, c0:c0 + FFN_COLS] * u2
        c = c + cw_ref[1:2, c0:c0 + FFN_COLS] * u1
        c = c + cw_ref[2:3, c0:c0 + FFN_COLS] * u
        gelu = 0.5 * c * (1.0 + lax.erf(c * sqrt_half))
        g_scr[s * ts:(s + 1) * ts, c0:c0 + FFN_COLS] = (gelu * v).astype(BF16)

    def down_dot(s):
        return jnp.dot(g_scr[s * ts:(s + 1) * ts, :], wdn_ref[...], preferred_element_type=F32)

    mixed = [mix_dot(s) for s in range(FFN_STREAMS)]
    h_mid, y = [None] * FFN_STREAMS, [None] * FFN_STREAMS
    for step in range(len(cols) + FFN_STREAMS - 1):
        for s in range(FFN_STREAMS):
            ci = step - s
            if ci == 0:
                h_mid[s], y[s] = mix_norm(s, mixed[s])
            if 0 <= ci < len(cols):
                ffn_chunk(s, y[s], cols[ci])
    down = [down_dot(s) for s in range(FFN_STREAMS)]
    for s in range(FFN_STREAMS):
        hout_ref[s * ts:(s + 1) * ts, :] = h_mid[s] + _rms(down[s], nw_ref[3:4])


def _layer_spec(stacked, layer):
    return pl.BlockSpec((None,) + stacked.shape[1:], lambda *_: (layer, 0, 0), pipeline_mode=pl.Buffered(1))


def _out_ffn(mix_list, h, w_out, nw, w_up, conv_w, conv_b, w_down, layer, seq):
    n, d = h.shape
    t = min(FFN_TILE, seq)
    tiles_per_seq = seq // t
    in_specs = [pl.BlockSpec((m.shape[0], t, LANES), lambda i: (0, i, 0)) for m in mix_list]
    in_specs += [
        pl.BlockSpec((t, d), lambda i: (i, 0)),
        _const_spec(w_out.shape),
        _layer_spec(nw, layer),
        _layer_spec(w_up, layer),
        _layer_spec(conv_w, layer),
        _layer_spec(conv_b, layer),
        _layer_spec(w_down, layer),
    ]
    return pl.pallas_call(
        functools.partial(_out_ffn_kernel, n_mix=len(mix_list), tiles_per_seq=tiles_per_seq),
        grid=(n // t,),
        in_specs=in_specs,
        out_specs=pl.BlockSpec((t, d), lambda i: (i, 0)),
        out_shape=jax.ShapeDtypeStruct((n, d), F32),
        scratch_shapes=[pltpu.VMEM((t, D_FF), BF16),
                        pltpu.VMEM((FFN_STREAMS, t // FFN_STREAMS + CARRY_ROWS, FFN_COLS), F32),
                        pltpu.VMEM((CARRY_ROWS, D_FF), F32)],
        compiler_params=pltpu.CompilerParams(
            dimension_semantics=("arbitrary",), vmem_limit_bytes=VMEM_LIMIT_BYTES),
        name="out_ffn",
    )(*mix_list, h, w_out, nw, w_up, conv_w, conv_b, w_down)


def _rope_tables(seq):
    inv_freq = ROPE_THETA ** (-jnp.arange(0, HEAD_DIM, 2, dtype=F32) / HEAD_DIM)
    ang = jnp.arange(seq, dtype=F32)[:, None] * inv_freq[None, :]
    cos, sin = jnp.cos(ang), jnp.sin(ang)
    reps = LANES // HEAD_DIM
    cos_t = jnp.tile(jnp.concatenate([cos, cos], axis=-1), (1, reps))
    sin_t = jnp.tile(jnp.concatenate([-sin, sin], axis=-1), (1, reps))
    return cos_t, sin_t


_A_HEAD_ORDER = tuple(h for j in range(A_Q_HEADS // 2) for h in (j, j + A_Q_HEADS // 2))


def _rel_bias(table):
    pad = C_PREV_CHUNKS * CHUNK
    band = pad + CHUNK
    period = band + CHUNK
    w = np.arange(period)
    diff = np.where(w < band, w, w - period)
    idx = np.clip(pad - diff, -REL_CLIP, REL_CLIP) + REL_CLIP
    vals = table[:, idx]
    skew = jnp.tile(vals, (1, CHUNK))[:, :CHUNK * (period - 1)].reshape(-1, CHUNK, period - 1)
    return skew[:, :, :band]


def _head_cols(order):
    return np.concatenate([np.arange(h * HEAD_DIM, (h + 1) * HEAD_DIM) for h in order])


def kernel(x, even_w_in, even_w_out, even_sinks, hgrn_lb_logits, hgrn_norm_w, odd_w_in, odd_w_out,
           odd_rel_bias, ffn_w_up, ffn_conv_w, ffn_conv_b, ffn_w_down, norm_w):
    b, seq, d = x.shape
    depth = norm_w.shape[0]
    scale = HEAD_DIM ** -0.5 * LOG2E
    cos_t, sin_t = _rope_tables(seq)
    a_cols = _head_cols(_A_HEAD_ORDER)
    w_up_all = ffn_w_up.astype(BF16)
    w_down_all = ffn_w_down.astype(BF16)
    conv_b_all = ffn_conv_b[:, None, :]

    h = x.reshape(b * seq, d)
    for layer in range(depth):
        nw = norm_w[layer]
        j = layer // 2
        if layer % 2 == 0:
            w_in = even_w_in[j]
            w_in = jnp.concatenate([w_in[:, :A_Q_DIM][:, a_cols] * scale, w_in[:, A_Q_DIM:]], axis=1).astype(BF16)
            w_out = jnp.concatenate([even_w_out[j][:A_Q_DIM][a_cols], even_w_out[j][A_Q_DIM:]], axis=0).astype(BF16)
            rope_groups = (A_Q_DIM + A_KV_DIM) // LANES
            proj = _in_proj(h, nw[0:1], w_in, cos_t, sin_t, rope_groups, seq).reshape(-1, b, seq, LANES)
            tq = min(ATTN_A_TQ, seq)
            sinks = jnp.repeat(even_sinks[j][np.array(_A_HEAD_ORDER)].astype(F32) * LOG2E, tq)[None, None, :]
            n_groups = A_Q_DIM // LANES
            mask = jnp.asarray(np.tile(_band_mask(tq, A_PREV_CHUNKS).T, (1, 2 * n_groups))[None, None])
            oa = _band_attention(proj, mask, q_col=0, k_col=n_groups, v_col=n_groups + 1, n_kv=1,
                                 groups_per_kv=n_groups, units_per_step=1, n_prev=A_PREV_CHUNKS, tq=tq,
                                 sinks=sinks)
            c0 = (A_Q_DIM + 2 * A_KV_DIM) // LANES
            ob = _hgrn(proj, hgrn_lb_logits, hgrn_norm_w[j:j + 1],
                       q_col=c0, f_col=c0 + B_HEADS, i_col=c0 + 2 * B_HEADS, g_col=c0 + 3 * B_HEADS, layer_j=j)
            mix = [oa.reshape(-1, b * seq, LANES), ob.reshape(-1, b * seq, LANES)]
        else:
            w_in = odd_w_in[j]
            w_in = jnp.concatenate([w_in[:, :C_DIM] * scale, w_in[:, C_DIM:]], axis=1).astype(BF16)
            w_out = odd_w_out[j].astype(BF16)
            proj = _in_proj(h, nw[0:1], w_in, cos_t, sin_t, 0, seq).reshape(-1, b, seq, LANES)
            pad = C_PREV_CHUNKS * CHUNK
            n_groups = C_DIM // LANES
            tq = min(ATTN_C_TQ, seq)
            n_qc = tq // CHUNK
            rel_bias = _rel_bias(odd_rel_bias[j].astype(F32) * LOG2E)
            bias = jnp.stack([jnp.pad(rel_bias, ((0, 0), (0, 0), (qc * CHUNK, (n_qc - 1 - qc) * CHUNK)),
                                      constant_values=-jnp.inf) for qc in range(n_qc)], axis=1)
            bias = jnp.swapaxes(bias.reshape(n_groups, 2 * tq, tq + pad), 1, 2)
            bias = bias.reshape(n_groups // ATTN_C_UNITS, ATTN_C_UNITS, tq + pad, 2 * tq)
            oc = _band_attention(proj, bias, q_col=0, k_col=n_groups, v_col=2 * n_groups, n_kv=n_groups,
                                 groups_per_kv=1, units_per_step=ATTN_C_UNITS, n_prev=C_PREV_CHUNKS, tq=tq)
            mix = [oc.reshape(-1, b * seq, LANES)]
        h = _out_ffn(mix, h, w_out, norm_w, w_up_all, ffn_conv_w, conv_b_all, w_down_all, layer, seq)
    return h.reshape(b, seq, d)
```

```python
import functools

import numpy as np
import jax
import jax.numpy as jnp
from jax import lax
from jax.experimental import pallas as pl
from jax.experimental.pallas import tpu as pltpu

F32 = jnp.float32
BF16 = jnp.bfloat16

D_MODEL = 1024
CHUNK = 64
HEAD_DIM = 64
ROPE_THETA = 10000.0
EPS = 1e-6
A_Q_HEADS = 8
A_KV_HEADS = 2
A_PREV_CHUNKS = 2
B_KEY_DIM = 128
B_VAL_DIM = 128
B_HEADS = 4
B_BLOCK = 32
C_HEADS = 16
C_PREV_CHUNKS = 8
REL_CLIP = 128
D_FF = 2816
CONV_WIDTH = 3
A_Q_DIM = A_Q_HEADS * HEAD_DIM
A_KV_DIM = A_KV_HEADS * HEAD_DIM
B_QK_DIM = B_HEADS * B_KEY_DIM
B_V_DIM = B_HEADS * B_VAL_DIM
EVEN_IN = A_Q_DIM + 2 * A_KV_DIM + 2 * B_QK_DIM + 2 * B_V_DIM
C_DIM = C_HEADS * HEAD_DIM
ODD_IN = 3 * C_DIM

LANES = 128
VMEM_LIMIT_BYTES = 56 * 1024 * 1024

PROJ_TILE = 1024
FFN_TILE = 512
PROJ_COLS = 256
PROJ_STREAMS = 2
FFN_COLS = 256
FFN_STREAMS = 2
HGRN_TILE = 256
HGRN_HEADS_PER_STEP = 2
ATTN_A_TQ = 128
ATTN_C_TQ = 128
ATTN_C_UNITS = 4
ONES_ROWS = 16
LOG2E = float(np.log2(np.e))
CARRY_ROWS = 8


def _rms(x, w):
    return x * lax.rsqrt(jnp.mean(x * x, axis=-1, keepdims=True) + EPS) * w


def _const_spec(shape):
    nd = len(shape)
    return pl.BlockSpec(shape, lambda *_: (0,) * nd, pipeline_mode=pl.Buffered(1))


def _in_proj_kernel(h_ref, nw_ref, w_ref, cos_ref, sin_ref, o_ref, *, rope_groups):
    n_out = w_ref.shape[1]
    ts = h_ref.shape[0] // PROJ_STREAMS
    lane = lax.broadcasted_iota(jnp.int32, (1, LANES), 1)
    first_half = (lane % HEAD_DIM) < (HEAD_DIM // 2)
    cols = list(range(0, n_out, PROJ_COLS))

    def project(s, y, c0):
        rows = slice(s * ts, (s + 1) * ts)
        acc = jnp.dot(y, w_ref[:, c0:c0 + PROJ_COLS], preferred_element_type=F32)
        for g0 in range(0, PROJ_COLS, LANES):
            part = acc[:, g0:g0 + LANES]
            if (c0 + g0) // LANES < rope_groups:
                partner = jnp.where(first_half, pltpu.roll(part, LANES - HEAD_DIM // 2, 1),
                                    pltpu.roll(part, HEAD_DIM // 2, 1))
                part = part * cos_ref[rows, :] + partner * sin_ref[rows, :]
            o_ref[(c0 + g0) // LANES, rows, :] = part.astype(BF16)

    y = [None] * PROJ_STREAMS
    for step in range(len(cols) + PROJ_STREAMS - 1):
        for s in range(PROJ_STREAMS):
            ci = step - s
            if ci == 0:
                y[s] = _rms(h_ref[s * ts:(s + 1) * ts, :], nw_ref[...]).astype(BF16)
            if 0 <= ci < len(cols):
                project(s, y[s], cols[ci])


def _in_proj(h, nw, w, cos, sin, rope_groups, seq):
    n, d = h.shape
    n_out = w.shape[1]
    n_grp = n_out // LANES
    t = min(PROJ_TILE, seq)
    tiles_per_seq = seq // t
    return pl.pallas_call(
        functools.partial(_in_proj_kernel, rope_groups=rope_groups),
        grid=(n // t,),
        in_specs=[
            pl.BlockSpec((t, d), lambda i: (i, 0)),
            _const_spec((1, d)),
            _const_spec((d, n_out)),
            pl.BlockSpec((t, LANES), lambda i: (i % tiles_per_seq, 0)),
            pl.BlockSpec((t, LANES), lambda i: (i % tiles_per_seq, 0)),
        ],
        out_specs=pl.BlockSpec((n_grp, t, LANES), lambda i: (0, i, 0)),
        out_shape=jax.ShapeDtypeStruct((n_grp, n, LANES), BF16),
        compiler_params=pltpu.CompilerParams(
            dimension_semantics=("arbitrary",), vmem_limit_bytes=VMEM_LIMIT_BYTES),
        name="in_proj",
    )(h, nw, w, cos, sin)


def _band_attention_kernel(*refs, n_units, n_groups, n_prev, tq, has_sinks):
    q_ref, k_ref, v_ref, bias_ref = refs[:4]
    sink_ref = refs[4] if has_sinks else None
    o_ref, kpad, vtpad = refs[4 + has_sinks:]

    seq = q_ref.shape[2]
    pad = n_prev * CHUNK
    tk = pad + tq
    n_tiles = seq // tq

    row_major_pv = not has_sinks
    for u in range(n_units):
        kpad[u, 0:pad, :] = jnp.zeros((pad, LANES), BF16)
        kpad[u, pad:pad + seq, :] = k_ref[u, 0]
        if row_major_pv:
            vtpad[u, 0:pad, 0:LANES] = jnp.zeros((pad, LANES), BF16)
            vtpad[u, pad:pad + seq, 0:LANES] = v_ref[u, 0]
            vtpad[u, :, LANES:2 * LANES] = jnp.ones((pad + seq, LANES), BF16)
        else:
            vtpad[u, 0:LANES, 0:pad] = jnp.zeros((LANES, pad), BF16)
            vtpad[u, 0:LANES, pad:pad + seq] = v_ref[u, 0].astype(F32).T.astype(BF16)
            vtpad[u, LANES:LANES + ONES_ROWS, :] = jnp.ones((ONES_ROWS, pad + seq), BF16)

    lane = lax.broadcasted_iota(jnp.int32, (1, LANES), 1)
    low = lane < HEAD_DIM
    key_idx = lax.broadcasted_iota(jnp.int32, (tk, 1), 0)
    zero = jnp.zeros((), BF16)

    def scores(u, ti):
        r0 = ti * tq
        parts = []
        for j in range(n_groups):
            qj = q_ref[u * n_groups + j, 0, r0:r0 + tq, :]
            parts.append(jnp.where(low, qj, zero))
            parts.append(jnp.where(low, zero, qj))
        rows = jnp.concatenate(parts, axis=0)
        s = lax.dot_general(kpad[u, r0:r0 + tk, :], rows, (((1,), (1,)), ((), ())),
                            preferred_element_type=F32)
        s = s + bias_ref[0, u]
        if r0 < pad:
            s = jnp.where(key_idx >= pad - r0, s, -jnp.inf)
        return s

    def row_max(u, s):
        m = jnp.max(s, axis=0, keepdims=True)
        if has_sinks:
            m = jnp.maximum(m, sink_ref[0, u:u + 1])
        return s, m

    def softmax(s, m):
        return jnp.exp2(s - m).astype(BF16), m

    def finish(u, ti, p, m):
        r0 = ti * tq
        if row_major_pv:
            o = lax.dot_general(p, vtpad[u, r0:r0 + tk, :], (((0,), (0,)), ((), ())),
                                preferred_element_type=F32)
            o = o[:, 0:LANES] * (1.0 / o[:, LANES:2 * LANES])
            for j in range(n_groups):
                top = o[2 * j * tq:(2 * j + 1) * tq]
                bot = o[(2 * j + 1) * tq:(2 * j + 2) * tq]
                o_ref[u * n_groups + j, 0, r0:r0 + tq, :] = jnp.where(low, top, bot).astype(BF16)
            return
        o = jnp.dot(vtpad[u, :, r0:r0 + tk], p, preferred_element_type=F32)
        denom = o[LANES:LANES + 1]
        if has_sinks:
            denom = denom + jnp.exp2(sink_ref[0, u:u + 1] - m)
        o = o[0:LANES] * (1.0 / denom)
        for j in range(n_groups):
            top = o[0:HEAD_DIM, 2 * j * tq:(2 * j + 1) * tq]
            bot = o[HEAD_DIM:LANES, (2 * j + 1) * tq:(2 * j + 2) * tq]
            o_ref[u * n_groups + j, 0, r0:r0 + tq, :] = jnp.concatenate([top, bot], axis=0).T.astype(BF16)

    items = [(u, ti) for u in range(n_units) for ti in range(n_tiles)]
    s_buf, m_buf, p_buf = {}, {}, {}
    for step in range(len(items) + 3):
        if step < len(items):
            s_buf[step] = scores(*items[step])
        if 0 <= step - 1 < len(items):
            m_buf[step - 1] = row_max(items[step - 1][0], s_buf.pop(step - 1))
        if 0 <= step - 2 < len(items):
            p_buf[step - 2] = softmax(*m_buf.pop(step - 2))
        if 0 <= step - 3 < len(items):
            finish(*items[step - 3], *p_buf.pop(step - 3))


def _band_mask(tq, n_prev):
    qc = np.arange(tq)[:, None] // CHUNK
    kc = np.arange(tq + n_prev * CHUNK)[None, :] // CHUNK
    return np.where((kc >= qc) & (kc <= qc + n_prev), 0.0, -np.inf).astype(np.float32)


def _band_attention(proj, bias, *, q_col, k_col, v_col, n_kv, groups_per_kv, units_per_step, n_prev, tq,
                    sinks=None):
    _, b, seq, _ = proj.shape
    n_steps = n_kv // units_per_step
    q_blk = units_per_step * groups_per_kv
    tk = tq + n_prev * CHUNK
    rows = groups_per_kv * 2 * tq
    in_specs = [
        pl.BlockSpec((q_blk, 1, seq, LANES), lambda j, i: (q_col // q_blk + j, i, 0, 0)),
        pl.BlockSpec((units_per_step, 1, seq, LANES), lambda j, i: (k_col // units_per_step + j, i, 0, 0)),
        pl.BlockSpec((units_per_step, 1, seq, LANES), lambda j, i: (v_col // units_per_step + j, i, 0, 0)),
        pl.BlockSpec((1, units_per_step, tk, rows), lambda j, i: (j, 0, 0, 0)),
    ]
    args = [proj, proj, proj, bias]
    if sinks is not None:
        in_specs.append(pl.BlockSpec((1, units_per_step, rows), lambda j, i: (j, 0, 0)))
        args.append(sinks)
    return pl.pallas_call(
        functools.partial(_band_attention_kernel, n_units=units_per_step, n_groups=groups_per_kv,
                          n_prev=n_prev, tq=tq, has_sinks=sinks is not None),
        grid=(n_steps, b),
        in_specs=in_specs,
        out_specs=pl.BlockSpec((q_blk, 1, seq, LANES), lambda j, i: (j, i, 0, 0)),
        out_shape=jax.ShapeDtypeStruct((n_kv * groups_per_kv, b, seq, LANES), BF16),
        scratch_shapes=[pltpu.VMEM((units_per_step, seq + n_prev * CHUNK, LANES), BF16),
                        pltpu.VMEM((units_per_step, seq + n_prev * CHUNK, 2 * LANES) if sinks is None else
                                   (units_per_step, LANES + ONES_ROWS, seq + n_prev * CHUNK), BF16)],
        compiler_params=pltpu.CompilerParams(
            dimension_semantics=("arbitrary", "arbitrary"), vmem_limit_bytes=VMEM_LIMIT_BYTES),
        name="band_attention",
    )(*args)


def _hgrn_kernel(q_ref, f_ref, i_ref, g_ref, lbl_ref, nw_ref, o_ref, *, layer_j):
    n_heads, seq, _ = q_ref.shape
    t = min(HGRN_TILE, seq)
    n_tiles = seq // t
    n_blk = t // B_BLOCK

    logits = lbl_ref[...]
    e = jnp.exp(logits - jnp.max(logits, axis=0, keepdims=True))
    sm = e / jnp.sum(e, axis=0, keepdims=True)
    lb = sm[0:1]
    for r in range(1, layer_j + 1):
        lb = lb + sm[r:r + 1]

    row = lax.broadcasted_iota(jnp.int32, (t, t), 0)
    col = lax.broadcasted_iota(jnp.int32, (t, t), 1)
    same_blk = (row // B_BLOCK) == (col // B_BLOCK)
    tril = jnp.logical_and(same_blk, col <= row)
    cum_mat = jnp.where(tril, 1.0, 0.0).astype(BF16)
    nw = nw_ref[...]

    def block_diag(x):
        cols = []
        for n in range(n_blk):
            pieces = [x[n * B_BLOCK:(n + 1) * B_BLOCK]]
            if n > 0:
                pieces.insert(0, jnp.zeros((n * B_BLOCK, x.shape[1]), x.dtype))
            if n < n_blk - 1:
                pieces.append(jnp.zeros(((n_blk - 1 - n) * B_BLOCK, x.shape[1]), x.dtype))
            cols.append(jnp.concatenate(pieces, axis=0))
        return jnp.concatenate(cols, axis=1)

    def gate_part(hd, ti):
        r0 = ti * t
        lb_h = lb[:, hd * B_KEY_DIM:(hd + 1) * B_KEY_DIM]
        fl = f_ref[hd, r0:r0 + t, :].astype(F32)
        f = lb_h + (1.0 - lb_h) * jax.nn.sigmoid(fl)
        log_f = jnp.log(f)
        hi = log_f.astype(BF16)
        r1 = log_f - hi.astype(F32)
        mid = r1.astype(BF16)
        lo = (r1 - mid.astype(F32)).astype(BF16)
        parts = jnp.dot(cum_mat, jnp.concatenate([hi, mid, lo], axis=1), preferred_element_type=F32)
        cum = (parts[:, 0:LANES] + parts[:, LANES:2 * LANES]) + parts[:, 2 * LANES:3 * LANES]
        last = jnp.broadcast_to(cum.reshape(n_blk, B_BLOCK, B_KEY_DIM)[:, B_BLOCK - 1:B_BLOCK, :],
                                (n_blk, B_BLOCK, B_KEY_DIM)).reshape(t, B_KEY_DIM)
        return 1.0 - f, cum, last

    def decay_part(hd, ti, k, cum, last):
        r0 = ti * t
        q = q_ref[hd, r0:r0 + t, :].astype(F32)
        q_dec = (q * jnp.exp(cum)).astype(BF16)
        k_dec = (k * jnp.exp(-cum)).astype(BF16)
        k_end = (k * jnp.exp(last - cum)).astype(BF16)
        blk_decay = jnp.exp(last)
        v = i_ref[hd, r0:r0 + t, :]
        s = lax.dot_general(q_dec, k_dec, (((1,), (1,)), ((), ())), preferred_element_type=F32)
        s = jnp.where(tril, s, 0.0)
        o_intra = jnp.dot(s.astype(BF16), v, preferred_element_type=F32)
        upd = lax.dot_general(v, block_diag(k_end), (((0,), (0,)), ((), ())), preferred_element_type=F32)
        return q_dec, blk_decay, upd, o_intra

    def state_part(hd, ti, state_t, q_dec, blk_decay, upd, o_intra):
        r0 = ti * t
        o_inter = []
        for n in range(n_blk):
            o_inter.append(lax.dot_general(q_dec[n * B_BLOCK:(n + 1) * B_BLOCK], state_t.astype(BF16),
                                           (((1,), (1,)), ((), ())), preferred_element_type=F32))
            state_t = (blk_decay[n * B_BLOCK:n * B_BLOCK + 1] * state_t
                       + upd[:, n * B_KEY_DIM:(n + 1) * B_KEY_DIM])
        return state_t, o_intra + jnp.concatenate(o_inter, axis=0)

    def output_part(hd, ti, o):
        r0 = ti * t
        o = _rms(o, nw[:, hd * B_VAL_DIM:(hd + 1) * B_VAL_DIM])
        g = g_ref[hd, r0:r0 + t, :].astype(F32)
        o_ref[hd, r0:r0 + t, :] = (o * (g * jax.nn.sigmoid(g))).astype(BF16)

    items = [(hd, ti) for hd in range(n_heads) for ti in range(n_tiles)]
    gates, decayed, mixed = {}, {}, {}
    state_t = None
    for step in range(len(items) + 3):
        if step < len(items):
            gates[step] = gate_part(*items[step])
        if 0 <= step - 1 < len(items):
            decayed[step - 1] = decay_part(*items[step - 1], *gates.pop(step - 1))
        if 0 <= step - 2 < len(items):
            hd, ti = items[step - 2]
            if ti == 0:
                state_t = jnp.zeros((B_VAL_DIM, B_KEY_DIM), F32)
            state_t, mixed[step - 2] = state_part(hd, ti, state_t, *decayed.pop(step - 2))
        if 0 <= step - 3 < len(items):
            output_part(*items[step - 3], mixed.pop(step - 3))


def _hgrn(proj, lb_logits, norm_w, *, q_col, f_col, i_col, g_col, layer_j):
    _, b, seq, _ = proj.shape
    n_layers = lb_logits.shape[0]

    hps = HGRN_HEADS_PER_STEP

    def col_spec(col):
        return pl.BlockSpec((hps, None, seq, LANES), lambda i, h: (col // hps + h, i, 0, 0))

    return pl.pallas_call(
        functools.partial(_hgrn_kernel, layer_j=layer_j),
        grid=(b, B_HEADS // hps),
        in_specs=[col_spec(q_col), col_spec(f_col), col_spec(i_col), col_spec(g_col),
                  pl.BlockSpec((n_layers, hps * LANES), lambda i, h: (0, h)),
                  pl.BlockSpec((1, hps * LANES), lambda i, h: (0, h))],
        out_specs=pl.BlockSpec((hps, None, seq, LANES), lambda i, h: (h, i, 0, 0)),
        out_shape=jax.ShapeDtypeStruct((B_HEADS, b, seq, LANES), BF16),
        compiler_params=pltpu.CompilerParams(
            dimension_semantics=("arbitrary", "arbitrary"), vmem_limit_bytes=VMEM_LIMIT_BYTES),
        name="hgrn2",
    )(proj, proj, proj, proj, lb_logits, norm_w)


def _out_ffn_kernel(*refs, n_mix, tiles_per_seq):
    mix_refs = refs[:n_mix]
    (h_ref, wout_ref, nw_ref, wup_ref, cw_ref, cb_ref, wdn_ref,
     hout_ref, g_scr, ubuf, carry) = refs[n_mix:]
    t = h_ref.shape[0]

    @pl.when(pl.program_id(0) % tiles_per_seq == 0)
    def _():
        carry[...] = jnp.zeros(carry.shape, F32)

    ts = t // FFN_STREAMS
    sqrt_half = np.float32(np.sqrt(0.5))
    cols = list(range(0, D_FF, FFN_COLS))

    def mix_dot(s):
        mix = jnp.concatenate([r[g, s * ts:(s + 1) * ts, :] for r in mix_refs for g in range(r.shape[0])], axis=1)
        return jnp.dot(mix, wout_ref[...], preferred_element_type=F32)

    def mix_norm(s, mixed):
        h_mid = h_ref[s * ts:(s + 1) * ts, :] + _rms(mixed, nw_ref[1:2])
        return h_mid, _rms(h_mid, nw_ref[2:3]).astype(BF16)

    def chunk_dots(y, c0):
        u = jnp.dot(y, wup_ref[:, c0:c0 + FFN_COLS], preferred_element_type=F32)
        v = jnp.dot(y, wup_ref[:, D_FF + c0:D_FF + c0 + FFN_COLS], preferred_element_type=F32)
        return u, v

    def chunk_gate(s, c0, u, v):
        ubuf[s, 0:CARRY_ROWS, :] = carry[:, c0:c0 + FFN_COLS]
        ubuf[s, CARRY_ROWS:CARRY_ROWS + ts, :] = u
        carry[:, c0:c0 + FFN_COLS] = u[ts - CARRY_ROWS:ts]
        u1 = ubuf[s, CARRY_ROWS - 1:CARRY_ROWS - 1 + ts, :]
        u2 = ubuf[s, CARRY_ROWS - 2:CARRY_ROWS - 2 + ts, :]
        c = cb_ref[:, c0:c0 + FFN_COLS] + cw_ref[0:1, c0:c0 + FFN_COLS] * u2
        c = c + cw_ref[1:2, c0:c0 + FFN_COLS] * u1
        c = c + cw_ref[2:3, c0:c0 + FFN_COLS] * u
        gelu = 0.5 * c * (1.0 + lax.erf(c * sqrt_half))
        g_scr[s * ts:(s + 1) * ts, c0:c0 + FFN_COLS] = (gelu * v).astype(BF16)

    def down_dot(s):
        return jnp.dot(g_scr[s * ts:(s + 1) * ts, :], wdn_ref[...], preferred_element_type=F32)

    mixed = [mix_dot(s) for s in range(FFN_STREAMS)]
    h_mid, y = [None] * FFN_STREAMS, [None] * FFN_STREAMS
    for step in range(len(cols) + FFN_STREAMS - 1):
        for s in range(FFN_STREAMS):
            ci = step - s
            if ci == 0:
                h_mid[s], y[s] = mix_norm(s, mixed[s])
            if 0 <= ci < len(cols):
                chunk_gate(s, cols[ci], *chunk_dots(y[s], cols[ci]))
    down = [down_dot(s) for s in range(FFN_STREAMS)]
    for s in range(FFN_STREAMS):
        hout_ref[s * ts:(s + 1) * ts, :] = h_mid[s] + _rms(down[s], nw_ref[3:4])


def _layer_spec(stacked, layer):
    return pl.BlockSpec((None,) + stacked.shape[1:], lambda *_: (layer, 0, 0), pipeline_mode=pl.Buffered(1))


def _out_ffn(mix_list, h, w_out, nw, w_up, conv_w, conv_b, w_down, layer, seq):
    n, d = h.shape
    t = min(FFN_TILE, seq)
    tiles_per_seq = seq // t
    in_specs = [pl.BlockSpec((m.shape[0], t, LANES), lambda i: (0, i, 0)) for m in mix_list]
    in_specs += [
        pl.BlockSpec((t, d), lambda i: (i, 0)),
        _const_spec(w_out.shape),
        _layer_spec(nw, layer),
        _layer_spec(w_up, layer),
        _layer_spec(conv_w, layer),
        _layer_spec(conv_b, layer),
        _layer_spec(w_down, layer),
    ]
    return pl.pallas_call(
        functools.partial(_out_ffn_kernel, n_mix=len(mix_list), tiles_per_seq=tiles_per_seq),
        grid=(n // t,),
        in_specs=in_specs,
        out_specs=pl.BlockSpec((t, d), lambda i: (i, 0)),
        out_shape=jax.ShapeDtypeStruct((n, d), F32),
        scratch_shapes=[pltpu.VMEM((t, D_FF), BF16),
                        pltpu.VMEM((FFN_STREAMS, t // FFN_STREAMS + CARRY_ROWS, FFN_COLS), F32),
                        pltpu.VMEM((CARRY_ROWS, D_FF), F32)],
        compiler_params=pltpu.CompilerParams(
            dimension_semantics=("arbitrary",), vmem_limit_bytes=VMEM_LIMIT_BYTES),
        name="out_ffn",
    )(*mix_list, h, w_out, nw, w_up, conv_w, conv_b, w_down)


def _rope_tables(seq):
    inv_freq = ROPE_THETA ** (-jnp.arange(0, HEAD_DIM, 2, dtype=F32) / HEAD_DIM)
    ang = jnp.arange(seq, dtype=F32)[:, None] * inv_freq[None, :]
    cos, sin = jnp.cos(ang), jnp.sin(ang)
    reps = LANES // HEAD_DIM
    cos_t = jnp.tile(jnp.concatenate([cos, cos], axis=-1), (1, reps))
    sin_t = jnp.tile(jnp.concatenate([-sin, sin], axis=-1), (1, reps))
    return cos_t, sin_t


_A_HEAD_ORDER = tuple(h for j in range(A_Q_HEADS // 2) for h in (j, j + A_Q_HEADS // 2))


def _rel_bias(table):
    pad = C_PREV_CHUNKS * CHUNK
    band = pad + CHUNK
    period = band + CHUNK
    w = np.arange(period)
    diff = np.where(w < band, w, w - period)
    idx = np.clip(pad - diff, -REL_CLIP, REL_CLIP) + REL_CLIP
    vals = table[:, idx]
    skew = jnp.tile(vals, (1, CHUNK))[:, :CHUNK * (period - 1)].reshape(-1, CHUNK, period - 1)
    return skew[:, :, :band]


def _head_cols(order):
    return np.concatenate([np.arange(h * HEAD_DIM, (h + 1) * HEAD_DIM) for h in order])


def kernel(x, even_w_in, even_w_out, even_sinks, hgrn_lb_logits, hgrn_norm_w, odd_w_in, odd_w_out,
           odd_rel_bias, ffn_w_up, ffn_conv_w, ffn_conv_b, ffn_w_down, norm_w):
    b, seq, d = x.shape
    depth = norm_w.shape[0]
    scale = HEAD_DIM ** -0.5 * LOG2E
    cos_t, sin_t = _rope_tables(seq)
    a_cols = _head_cols(_A_HEAD_ORDER)
    w_up_all = ffn_w_up.astype(BF16)
    w_down_all = ffn_w_down.astype(BF16)
    conv_b_all = ffn_conv_b[:, None, :]

    h = x.reshape(b * seq, d)
    for layer in range(depth):
        nw = norm_w[layer]
        j = layer // 2
        if layer % 2 == 0:
            w_in = even_w_in[j]
            w_in = jnp.concatenate([w_in[:, :A_Q_DIM][:, a_cols] * scale, w_in[:, A_Q_DIM:]], axis=1).astype(BF16)
            w_out = jnp.concatenate([even_w_out[j][:A_Q_DIM][a_cols], even_w_out[j][A_Q_DIM:]], axis=0).astype(BF16)
            rope_groups = (A_Q_DIM + A_KV_DIM) // LANES
            proj = _in_proj(h, nw[0:1], w_in, cos_t, sin_t, rope_groups, seq).reshape(-1, b, seq, LANES)
            tq = min(ATTN_A_TQ, seq)
            sinks = jnp.repeat(even_sinks[j][np.array(_A_HEAD_ORDER)].astype(F32) * LOG2E, tq)[None, None, :]
            n_groups = A_Q_DIM // LANES
            mask = jnp.asarray(np.tile(_band_mask(tq, A_PREV_CHUNKS).T, (1, 2 * n_groups))[None, None])
            oa = _band_attention(proj, mask, q_col=0, k_col=n_groups, v_col=n_groups + 1, n_kv=1,
                                 groups_per_kv=n_groups, units_per_step=1, n_prev=A_PREV_CHUNKS, tq=tq,
                                 sinks=sinks)
            c0 = (A_Q_DIM + 2 * A_KV_DIM) // LANES
            ob = _hgrn(proj, hgrn_lb_logits, hgrn_norm_w[j:j + 1],
                       q_col=c0, f_col=c0 + B_HEADS, i_col=c0 + 2 * B_HEADS, g_col=c0 + 3 * B_HEADS, layer_j=j)
            mix = [oa.reshape(-1, b * seq, LANES), ob.reshape(-1, b * seq, LANES)]
        else:
            w_in = odd_w_in[j]
            w_in = jnp.concatenate([w_in[:, :C_DIM] * scale, w_in[:, C_DIM:]], axis=1).astype(BF16)
            w_out = odd_w_out[j].astype(BF16)
            proj = _in_proj(h, nw[0:1], w_in, cos_t, sin_t, 0, seq).reshape(-1, b, seq, LANES)
            pad = C_PREV_CHUNKS * CHUNK
            n_groups = C_DIM // LANES
            tq = min(ATTN_C_TQ, seq)
            n_qc = tq // CHUNK
            rel_bias = _rel_bias(odd_rel_bias[j].astype(F32) * LOG2E)
            bias = jnp.stack([jnp.pad(rel_bias, ((0, 0), (0, 0), (qc * CHUNK, (n_qc - 1 - qc) * CHUNK)),
                                      constant_values=-jnp.inf) for qc in range(n_qc)], axis=1)
            bias = jnp.swapaxes(bias.reshape(n_groups, 2 * tq, tq + pad), 1, 2)
            bias = bias.reshape(n_groups // ATTN_C_UNITS, ATTN_C_UNITS, tq + pad, 2 * tq)
            oc = _band_attention(proj, bias, q_col=0, k_col=n_groups, v_col=2 * n_groups, n_kv=n_groups,
                                 groups_per_kv=1, units_per_step=ATTN_C_UNITS, n_prev=C_PREV_CHUNKS, tq=tq)
            mix = [oc.reshape(-1, b * seq, LANES)]
        h = _out_ffn(mix, h, w_out, norm_w, w_up_all, ffn_conv_w, conv_b_all, w_down_all, layer, seq)
    return h.reshape(b, seq, d)
```

```python
import functools

import numpy as np
import jax
import jax.numpy as jnp
from jax import lax
from jax.experimental import pallas as pl
from jax.experimental.pallas import tpu as pltpu

F32 = jnp.float32
BF16 = jnp.bfloat16

D_MODEL = 1024
CHUNK = 64
HEAD_DIM = 64
ROPE_THETA = 10000.0
EPS = 1e-6
A_Q_HEADS = 8
A_KV_HEADS = 2
A_PREV_CHUNKS = 2
B_KEY_DIM = 128
B_VAL_DIM = 128
B_HEADS = 4
B_BLOCK = 32
C_HEADS = 16
C_PREV_CHUNKS = 8
REL_CLIP = 128
D_FF = 2816
CONV_WIDTH = 3
A_Q_DIM = A_Q_HEADS * HEAD_DIM
A_KV_DIM = A_KV_HEADS * HEAD_DIM
B_QK_DIM = B_HEADS * B_KEY_DIM
B_V_DIM = B_HEADS * B_VAL_DIM
EVEN_IN = A_Q_DIM + 2 * A_KV_DIM + 2 * B_QK_DIM + 2 * B_V_DIM
C_DIM = C_HEADS * HEAD_DIM
ODD_IN = 3 * C_DIM

LANES = 128
VMEM_LIMIT_BYTES = 56 * 1024 * 1024

PROJ_TILE = 1024
FFN_TILE = 512
PROJ_COLS = 256
PROJ_STREAMS = 2
FFN_COLS = 256
FFN_STREAMS = 2
HGRN_TILE = 256
HGRN_HEADS_PER_STEP = 2
ATTN_A_TQ = 128
ATTN_C_TQ = 128
ATTN_C_UNITS = 4
ONES_ROWS = 16
LOG2E = float(np.log2(np.e))
CARRY_ROWS = 8


def _rms(x, w):
    return x * lax.rsqrt(jnp.mean(x * x, axis=-1, keepdims=True) + EPS) * w


def _const_spec(shape):
    nd = len(shape)
    return pl.BlockSpec(shape, lambda *_: (0,) * nd, pipeline_mode=pl.Buffered(1))


def _in_proj_kernel(h_ref, nw_ref, w_ref, cos_ref, sin_ref, o_ref, *, rope_groups):
    n_out = w_ref.shape[1]
    ts = h_ref.shape[0] // PROJ_STREAMS
    lane = lax.broadcasted_iota(jnp.int32, (1, LANES), 1)
    first_half = (lane % HEAD_DIM) < (HEAD_DIM // 2)
    cols = list(range(0, n_out, PROJ_COLS))

    def project(s, y, c0):
        rows = slice(s * ts, (s + 1) * ts)
        acc = jnp.dot(y, w_ref[:, c0:c0 + PROJ_COLS], preferred_element_type=F32)
        for g0 in range(0, PROJ_COLS, LANES):
            part = acc[:, g0:g0 + LANES]
            if (c0 + g0) // LANES < rope_groups:
                partner = jnp.where(first_half, pltpu.roll(part, LANES - HEAD_DIM // 2, 1),
                                    pltpu.roll(part, HEAD_DIM // 2, 1))
                part = part * cos_ref[rows, :] + partner * sin_ref[rows, :]
            o_ref[(c0 + g0) // LANES, rows, :] = part.astype(BF16)

    y = [None] * PROJ_STREAMS
    for step in range(len(cols) + PROJ_STREAMS - 1):
        for s in range(PROJ_STREAMS):
            ci = step - s
            if ci == 0:
                y[s] = _rms(h_ref[s * ts:(s + 1) * ts, :], nw_ref[...]).astype(BF16)
            if 0 <= ci < len(cols):
                project(s, y[s], cols[ci])


def _in_proj(h, nw, w, cos, sin, rope_groups, seq):
    n, d = h.shape
    n_out = w.shape[1]
    n_grp = n_out // LANES
    t = min(PROJ_TILE, seq)
    tiles_per_seq = seq // t
    return pl.pallas_call(
        functools.partial(_in_proj_kernel, rope_groups=rope_groups),
        grid=(n // t,),
        in_specs=[
            pl.BlockSpec((t, d), lambda i: (i, 0)),
            _const_spec((1, d)),
            _const_spec((d, n_out)),
            pl.BlockSpec((t, LANES), lambda i: (i % tiles_per_seq, 0)),
            pl.BlockSpec((t, LANES), lambda i: (i % tiles_per_seq, 0)),
        ],
        out_specs=pl.BlockSpec((n_grp, t, LANES), lambda i: (0, i, 0)),
        out_shape=jax.ShapeDtypeStruct((n_grp, n, LANES), BF16),
        compiler_params=pltpu.CompilerParams(
            dimension_semantics=("arbitrary",), vmem_limit_bytes=VMEM_LIMIT_BYTES),
        name="in_proj",
    )(h, nw, w, cos, sin)


def _band_attention_kernel(*refs, n_units, n_groups, n_prev, tq, has_sinks):
    q_ref, k_ref, v_ref, bias_ref = refs[:4]
    sink_ref = refs[4] if has_sinks else None
    o_ref, kpad, vtpad = refs[4 + has_sinks:]

    seq = q_ref.shape[2]
    pad = n_prev * CHUNK
    tk = pad + tq
    n_tiles = seq // tq

    row_major_pv = not has_sinks
    for u in range(n_units):
        kpad[u, 0:pad, :] = jnp.zeros((pad, LANES), BF16)
        kpad[u, pad:pad + seq, :] = k_ref[u, 0]
        if row_major_pv:
            vtpad[u, 0:pad, 0:LANES] = jnp.zeros((pad, LANES), BF16)
            vtpad[u, pad:pad + seq, 0:LANES] = v_ref[u, 0]
            vtpad[u, :, LANES:2 * LANES] = jnp.ones((pad + seq, LANES), BF16)
        else:
            vtpad[u, 0:LANES, 0:pad] = jnp.zeros((LANES, pad), BF16)
            vtpad[u, 0:LANES, pad:pad + seq] = v_ref[u, 0].astype(F32).T.astype(BF16)
            vtpad[u, LANES:LANES + ONES_ROWS, :] = jnp.ones((ONES_ROWS, pad + seq), BF16)

    lane = lax.broadcasted_iota(jnp.int32, (1, LANES), 1)
    low = lane < HEAD_DIM
    key_idx = lax.broadcasted_iota(jnp.int32, (tk, 1), 0)
    zero = jnp.zeros((), BF16)

    def scores(u, ti):
        r0 = ti * tq
        parts = []
        for j in range(n_groups):
            qj = q_ref[u * n_groups + j, 0, r0:r0 + tq, :]
            parts.append(jnp.where(low, qj, zero))
            parts.append(jnp.where(low, zero, qj))
        rows = jnp.concatenate(parts, axis=0)
        s = lax.dot_general(kpad[u, r0:r0 + tk, :], rows, (((1,), (1,)), ((), ())),
                            preferred_element_type=F32)
        s = s + bias_ref[0, u]
        if r0 < pad:
            s = jnp.where(key_idx >= pad - r0, s, -jnp.inf)
        return s

    def row_max(u, s):
        m = jnp.max(s, axis=0, keepdims=True)
        if has_sinks:
            m = jnp.maximum(m, sink_ref[0, u:u + 1])
        return s, m

    def softmax(s, m):
        return jnp.exp2(s - m).astype(BF16), m

    def finish(u, ti, p, m):
        r0 = ti * tq
        if row_major_pv:
            o = lax.dot_general(p, vtpad[u, r0:r0 + tk, :], (((0,), (0,)), ((), ())),
                                preferred_element_type=F32)
            o = o[:, 0:LANES] * (1.0 / o[:, LANES:2 * LANES])
            for j in range(n_groups):
                top = o[2 * j * tq:(2 * j + 1) * tq]
                bot = o[(2 * j + 1) * tq:(2 * j + 2) * tq]
                o_ref[u * n_groups + j, 0, r0:r0 + tq, :] = jnp.where(low, top, bot).astype(BF16)
            return
        o = jnp.dot(vtpad[u, :, r0:r0 + tk], p, preferred_element_type=F32)
        denom = o[LANES:LANES + 1]
        if has_sinks:
            denom = denom + jnp.exp2(sink_ref[0, u:u + 1] - m)
        o = o[0:LANES] * (1.0 / denom)
        for j in range(n_groups):
            top = o[0:HEAD_DIM, 2 * j * tq:(2 * j + 1) * tq]
            bot = o[HEAD_DIM:LANES, (2 * j + 1) * tq:(2 * j + 2) * tq]
            o_ref[u * n_groups + j, 0, r0:r0 + tq, :] = jnp.concatenate([top, bot], axis=0).T.astype(BF16)

    items = [(u, ti) for u in range(n_units) for ti in range(n_tiles)]
    s_buf, m_buf, p_buf = {}, {}, {}
    for step in range(len(items) + 3):
        if step < len(items):
            s_buf[step] = scores(*items[step])
        if 0 <= step - 1 < len(items):
            m_buf[step - 1] = row_max(items[step - 1][0], s_buf.pop(step - 1))
        if 0 <= step - 2 < len(items):
            p_buf[step - 2] = softmax(*m_buf.pop(step - 2))
        if 0 <= step - 3 < len(items):
            finish(*items[step - 3], *p_buf.pop(step - 3))


def _band_mask(tq, n_prev):
    qc = np.arange(tq)[:, None] // CHUNK
    kc = np.arange(tq + n_prev * CHUNK)[None, :] // CHUNK
    return np.where((kc >= qc) & (kc <= qc + n_prev), 0.0, -np.inf).astype(np.float32)


def _band_attention(proj, bias, *, q_col, k_col, v_col, n_kv, groups_per_kv, units_per_step, n_prev, tq,
                    sinks=None):
    _, b, seq, _ = proj.shape
    n_steps = n_kv // units_per_step
    q_blk = units_per_step * groups_per_kv
    tk = tq + n_prev * CHUNK
    rows = groups_per_kv * 2 * tq
    in_specs = [
        pl.BlockSpec((q_blk, 1, seq, LANES), lambda j, i: (q_col // q_blk + j, i, 0, 0)),
        pl.BlockSpec((units_per_step, 1, seq, LANES), lambda j, i: (k_col // units_per_step + j, i, 0, 0)),
        pl.BlockSpec((units_per_step, 1, seq, LANES), lambda j, i: (v_col // units_per_step + j, i, 0, 0)),
        pl.BlockSpec((1, units_per_step, tk, rows), lambda j, i: (j, 0, 0, 0)),
    ]
    args = [proj, proj, proj, bias]
    if sinks is not None:
        in_specs.append(pl.BlockSpec((1, units_per_step, rows), lambda j, i: (j, 0, 0)))
        args.append(sinks)
    return pl.pallas_call(
        functools.partial(_band_attention_kernel, n_units=units_per_step, n_groups=groups_per_kv,
                          n_prev=n_prev, tq=tq, has_sinks=sinks is not None),
        grid=(n_steps, b),
        in_specs=in_specs,
        out_specs=pl.BlockSpec((q_blk, 1, seq, LANES), lambda j, i: (j, i, 0, 0)),
        out_shape=jax.ShapeDtypeStruct((n_kv * groups_per_kv, b, seq, LANES), BF16),
        scratch_shapes=[pltpu.VMEM((units_per_step, seq + n_prev * CHUNK, LANES), BF16),
                        pltpu.VMEM((units_per_step, seq + n_prev * CHUNK, 2 * LANES) if sinks is None else
                                   (units_per_step, LANES + ONES_ROWS, seq + n_prev * CHUNK), BF16)],
        compiler_params=pltpu.CompilerParams(
            dimension_semantics=("arbitrary", "arbitrary"), vmem_limit_bytes=VMEM_LIMIT_BYTES),
        name="band_attention",
    )(*args)


def _hgrn_kernel(q_ref, f_ref, i_ref, g_ref, lbl_ref, nw_ref, o_ref, *, layer_j):
    n_heads, seq, _ = q_ref.shape
    t = min(HGRN_TILE, seq)
    n_tiles = seq // t
    n_blk = t // B_BLOCK

    logits = lbl_ref[...]
    e = jnp.exp(logits - jnp.max(logits, axis=0, keepdims=True))
    sm = e / jnp.sum(e, axis=0, keepdims=True)
    lb = sm[0:1]
    for r in range(1, layer_j + 1):
        lb = lb + sm[r:r + 1]

    row = lax.broadcasted_iota(jnp.int32, (t, t), 0)
    col = lax.broadcasted_iota(jnp.int32, (t, t), 1)
    same_blk = (row // B_BLOCK) == (col // B_BLOCK)
    tril = jnp.logical_and(same_blk, col <= row)
    cum_mat = jnp.where(tril, 1.0, 0.0).astype(BF16)
    nw = nw_ref[...]

    def block_diag(x):
        cols = []
        for n in range(n_blk):
            pieces = [x[n * B_BLOCK:(n + 1) * B_BLOCK]]
            if n > 0:
                pieces.insert(0, jnp.zeros((n * B_BLOCK, x.shape[1]), x.dtype))
            if n < n_blk - 1:
                pieces.append(jnp.zeros(((n_blk - 1 - n) * B_BLOCK, x.shape[1]), x.dtype))
            cols.append(jnp.concatenate(pieces, axis=0))
        return jnp.concatenate(cols, axis=1)

    def gate_part(hd, ti):
        r0 = ti * t
        lb_h = lb[:, hd * B_KEY_DIM:(hd + 1) * B_KEY_DIM]
        fl = f_ref[hd, r0:r0 + t, :].astype(F32)
        f = lb_h + (1.0 - lb_h) * jax.nn.sigmoid(fl)
        log_f = jnp.log(f)
        hi = log_f.astype(BF16)
        r1 = log_f - hi.astype(F32)
        mid = r1.astype(BF16)
        lo = (r1 - mid.astype(F32)).astype(BF16)
        parts = jnp.dot(cum_mat, jnp.concatenate([hi, mid, lo], axis=1), preferred_element_type=F32)
        cum = (parts[:, 0:LANES] + parts[:, LANES:2 * LANES]) + parts[:, 2 * LANES:3 * LANES]
        last = jnp.broadcast_to(cum.reshape(n_blk, B_BLOCK, B_KEY_DIM)[:, B_BLOCK - 1:B_BLOCK, :],
                                (n_blk, B_BLOCK, B_KEY_DIM)).reshape(t, B_KEY_DIM)
        return 1.0 - f, cum, last

    def decay_part(hd, ti, k, cum, last):
        r0 = ti * t
        q = q_ref[hd, r0:r0 + t, :].astype(F32)
        q_dec = (q * jnp.exp(cum)).astype(BF16)
        k_dec = (k * jnp.exp(-cum)).astype(BF16)
        k_end = (k * jnp.exp(last - cum)).astype(BF16)
        blk_decay = jnp.exp(last)
        v = i_ref[hd, r0:r0 + t, :]
        s = lax.dot_general(q_dec, k_dec, (((1,), (1,)), ((), ())), preferred_element_type=F32)
        s = jnp.where(tril, s, 0.0)
        o_intra = jnp.dot(s.astype(BF16), v, preferred_element_type=F32)
        upd = lax.dot_general(v, block_diag(k_end), (((0,), (0,)), ((), ())), preferred_element_type=F32)
        return q_dec, blk_decay, upd, o_intra

    def state_part(hd, ti, state_t, q_dec, blk_decay, upd, o_intra):
        r0 = ti * t
        o_inter = []
        for n in range(n_blk):
            o_inter.append(lax.dot_general(q_dec[n * B_BLOCK:(n + 1) * B_BLOCK], state_t.astype(BF16),
                                           (((1,), (1,)), ((), ())), preferred_element_type=F32))
            state_t = (blk_decay[n * B_BLOCK:n * B_BLOCK + 1] * state_t
                       + upd[:, n * B_KEY_DIM:(n + 1) * B_KEY_DIM])
        return state_t, o_intra + jnp.concatenate(o_inter, axis=0)

    def output_part(hd, ti, o):
        r0 = ti * t
        o = _rms(o, nw[:, hd * B_VAL_DIM:(hd + 1) * B_VAL_DIM])
        g = g_ref[hd, r0:r0 + t, :].astype(F32)
        o_ref[hd, r0:r0 + t, :] = (o * (g * jax.nn.sigmoid(g))).astype(BF16)

    items = [(hd, ti) for hd in range(n_heads) for ti in range(n_tiles)]
    gates, decayed, mixed = {}, {}, {}
    state_t = None
    for step in range(len(items) + 3):
        if step < len(items):
            gates[step] = gate_part(*items[step])
        if 0 <= step - 1 < len(items):
            decayed[step - 1] = decay_part(*items[step - 1], *gates.pop(step - 1))
        if 0 <= step - 2 < len(items):
            hd, ti = items[step - 2]
            if ti == 0:
                state_t = jnp.zeros((B_VAL_DIM, B_KEY_DIM), F32)
            state_t, mixed[step - 2] = state_part(hd, ti, state_t, *decayed.pop(step - 2))
        if 0 <= step - 3 < len(items):
            output_part(*items[step - 3], mixed.pop(step - 3))


def _hgrn(proj, lb_logits, norm_w, *, q_col, f_col, i_col, g_col, layer_j):
    _, b, seq, _ = proj.shape
    n_layers = lb_logits.shape[0]

    hps = HGRN_HEADS_PER_STEP

    def col_spec(col):
        return pl.BlockSpec((hps, None, seq, LANES), lambda i, h: (col // hps + h, i, 0, 0))

    return pl.pallas_call(
        functools.partial(_hgrn_kernel, layer_j=layer_j),
        grid=(b, B_HEADS // hps),
        in_specs=[col_spec(q_col), col_spec(f_col), col_spec(i_col), col_spec(g_col),
                  pl.BlockSpec((n_layers, hps * LANES), lambda i, h: (0, h)),
                  pl.BlockSpec((1, hps * LANES), lambda i, h: (0, h))],
        out_specs=pl.BlockSpec((hps, None, seq, LANES), lambda i, h: (h, i, 0, 0)),
        out_shape=jax.ShapeDtypeStruct((B_HEADS, b, seq, LANES), BF16),
        compiler_params=pltpu.CompilerParams(
            dimension_semantics=("arbitrary", "arbitrary"), vmem_limit_bytes=VMEM_LIMIT_BYTES),
        name="hgrn2",
    )(proj, proj, proj, proj, lb_logits, norm_w)


def _out_ffn_kernel(*refs, n_mix, tiles_per_seq, fuse_next):
    mix_refs = refs[:n_mix]
    h_ref, wout_ref, nw_ref, wup_ref, cw_ref, cb_ref, wdn_ref = refs[n_mix:n_mix + 7]
    rest = refs[n_mix + 7:]
    if fuse_next:
        nw_next_ref, w_next_ref, hout_ref, proj_next_ref, g_scr, ubuf, carry = rest
    else:
        hout_ref, g_scr, ubuf, carry = rest
    t = h_ref.shape[0]

    @pl.when(pl.program_id(0) % tiles_per_seq == 0)
    def _():
        carry[...] = jnp.zeros(carry.shape, F32)

    ts = t // FFN_STREAMS
    sqrt_half = np.float32(np.sqrt(0.5))
    cols = list(range(0, D_FF, FFN_COLS))

    def mix_dot(s):
        mix = jnp.concatenate([r[g, s * ts:(s + 1) * ts, :] for r in mix_refs for g in range(r.shape[0])], axis=1)
        return jnp.dot(mix, wout_ref[...], preferred_element_type=F32)

    def mix_norm(s, mixed):
        h_mid = h_ref[s * ts:(s + 1) * ts, :] + _rms(mixed, nw_ref[1:2])
        return h_mid, _rms(h_mid, nw_ref[2:3]).astype(BF16)

    def chunk_dots(y, c0):
        u = jnp.dot(y, wup_ref[:, c0:c0 + FFN_COLS], preferred_element_type=F32)
        v = jnp.dot(y, wup_ref[:, D_FF + c0:D_FF + c0 + FFN_COLS], preferred_element_type=F32)
        return u, v

    def chunk_gate(s, c0, u, v):
        ubuf[s, 0:CARRY_ROWS, :] = carry[:, c0:c0 + FFN_COLS]
        ubuf[s, CARRY_ROWS:CARRY_ROWS + ts, :] = u
        carry[:, c0:c0 + FFN_COLS] = u[ts - CARRY_ROWS:ts]
        u1 = ubuf[s, CARRY_ROWS - 1:CARRY_ROWS - 1 + ts, :]
        u2 = ubuf[s, CARRY_ROWS - 2:CARRY_ROWS - 2 + ts, :]
        c = cb_ref[:, c0:c0 + FFN_COLS] + cw_ref[0:1, c0:c0 + FFN_COLS] * u2
        c = c + cw_ref[1:2, c0:c0 + FFN_COLS] * u1
        c = c + cw_ref[2:3, c0:c0 + FFN_COLS] * u
        gelu = 0.5 * c * (1.0 + lax.erf(c * sqrt_half))
        g_scr[s * ts:(s + 1) * ts, c0:c0 + FFN_COLS] = (gelu * v).astype(BF16)

    def down_dot(s):
        return jnp.dot(g_scr[s * ts:(s + 1) * ts, :], wdn_ref[...], preferred_element_type=F32)

    mixed = [mix_dot(s) for s in range(FFN_STREAMS)]
    h_mid, y = [None] * FFN_STREAMS, [None] * FFN_STREAMS
    for step in range(len(cols) + FFN_STREAMS - 1):
        for s in range(FFN_STREAMS):
            ci = step - s
            if ci == 0:
                h_mid[s], y[s] = mix_norm(s, mixed[s])
            if 0 <= ci < len(cols):
                chunk_gate(s, cols[ci], *chunk_dots(y[s], cols[ci]))
    down = [down_dot(s) for s in range(FFN_STREAMS)]
    for s in range(FFN_STREAMS):
        h_new = h_mid[s] + _rms(down[s], nw_ref[3:4])
        hout_ref[s * ts:(s + 1) * ts, :] = h_new
        if fuse_next:
            y_next = _rms(h_new, nw_next_ref[...]).astype(BF16)
            for c0 in range(0, w_next_ref.shape[1], PROJ_COLS):
                acc = jnp.dot(y_next, w_next_ref[:, c0:c0 + PROJ_COLS], preferred_element_type=F32)
                for g0 in range(0, PROJ_COLS, LANES):
                    proj_next_ref[(c0 + g0) // LANES, s * ts:(s + 1) * ts, :] = acc[:, g0:g0 + LANES].astype(BF16)


def _layer_spec(stacked, layer):
    return pl.BlockSpec((None,) + stacked.shape[1:], lambda *_: (layer, 0, 0), pipeline_mode=pl.Buffered(1))


def _out_ffn(mix_list, h, w_out, nw, w_up, conv_w, conv_b, w_down, layer, seq, next_proj=None):
    n, d = h.shape
    t = min(FFN_TILE, seq)
    tiles_per_seq = seq // t
    in_specs = [pl.BlockSpec((m.shape[0], t, LANES), lambda i: (0, i, 0)) for m in mix_list]
    in_specs += [
        pl.BlockSpec((t, d), lambda i: (i, 0)),
        _const_spec(w_out.shape),
        _layer_spec(nw, layer),
        _layer_spec(w_up, layer),
        _layer_spec(conv_w, layer),
        _layer_spec(conv_b, layer),
        _layer_spec(w_down, layer),
    ]
    args = [*mix_list, h, w_out, nw, w_up, conv_w, conv_b, w_down]
    out_specs = pl.BlockSpec((t, d), lambda i: (i, 0))
    out_shape = jax.ShapeDtypeStruct((n, d), F32)
    if next_proj is not None:
        n_grp = next_proj[1].shape[1] // LANES
        in_specs += [_const_spec(next_proj[0].shape), _const_spec(next_proj[1].shape)]
        args += list(next_proj)
        out_specs = [out_specs, pl.BlockSpec((n_grp, t, LANES), lambda i: (0, i, 0))]
        out_shape = [out_shape, jax.ShapeDtypeStruct((n_grp, n, LANES), BF16)]
    return pl.pallas_call(
        functools.partial(_out_ffn_kernel, n_mix=len(mix_list), tiles_per_seq=tiles_per_seq,
                          fuse_next=next_proj is not None),
        grid=(n // t,),
        in_specs=in_specs,
        out_specs=out_specs,
        out_shape=out_shape,
        scratch_shapes=[pltpu.VMEM((t, D_FF), BF16),
                        pltpu.VMEM((FFN_STREAMS, t // FFN_STREAMS + CARRY_ROWS, FFN_COLS), F32),
                        pltpu.VMEM((CARRY_ROWS, D_FF), F32)],
        compiler_params=pltpu.CompilerParams(
            dimension_semantics=("arbitrary",), vmem_limit_bytes=VMEM_LIMIT_BYTES),
        name="out_ffn",
    )(*args)


def _rope_tables(seq):
    inv_freq = ROPE_THETA ** (-jnp.arange(0, HEAD_DIM, 2, dtype=F32) / HEAD_DIM)
    ang = jnp.arange(seq, dtype=F32)[:, None] * inv_freq[None, :]
    cos, sin = jnp.cos(ang), jnp.sin(ang)
    reps = LANES // HEAD_DIM
    cos_t = jnp.tile(jnp.concatenate([cos, cos], axis=-1), (1, reps))
    sin_t = jnp.tile(jnp.concatenate([-sin, sin], axis=-1), (1, reps))
    return cos_t, sin_t


_A_HEAD_ORDER = tuple(h for j in range(A_Q_HEADS // 2) for h in (j, j + A_Q_HEADS // 2))


def _rel_bias(table):
    pad = C_PREV_CHUNKS * CHUNK
    band = pad + CHUNK
    period = band + CHUNK
    w = np.arange(period)
    diff = np.where(w < band, w, w - period)
    idx = np.clip(pad - diff, -REL_CLIP, REL_CLIP) + REL_CLIP
    vals = table[:, idx]
    skew = jnp.tile(vals, (1, CHUNK))[:, :CHUNK * (period - 1)].reshape(-1, CHUNK, period - 1)
    return skew[:, :, :band]


def _head_cols(order):
    return np.concatenate([np.arange(h * HEAD_DIM, (h + 1) * HEAD_DIM) for h in order])


def kernel(x, even_w_in, even_w_out, even_sinks, hgrn_lb_logits, hgrn_norm_w, odd_w_in, odd_w_out,
           odd_rel_bias, ffn_w_up, ffn_conv_w, ffn_conv_b, ffn_w_down, norm_w):
    b, seq, d = x.shape
    depth = norm_w.shape[0]
    scale = HEAD_DIM ** -0.5 * LOG2E
    cos_t, sin_t = _rope_tables(seq)
    a_cols = _head_cols(_A_HEAD_ORDER)
    w_up_all = ffn_w_up.astype(BF16)
    w_down_all = ffn_w_down.astype(BF16)
    conv_b_all = ffn_conv_b[:, None, :]

    def odd_proj_weight(j):
        w_in = odd_w_in[j]
        return jnp.concatenate([w_in[:, :C_DIM] * scale, w_in[:, C_DIM:]], axis=1).astype(BF16)

    h = x.reshape(b * seq, d)
    fused_proj = None
    for layer in range(depth):
        nw = norm_w[layer]
        j = layer // 2
        if layer % 2 == 0:
            w_in = even_w_in[j]
            w_in = jnp.concatenate([w_in[:, :A_Q_DIM][:, a_cols] * scale, w_in[:, A_Q_DIM:]], axis=1).astype(BF16)
            w_out = jnp.concatenate([even_w_out[j][:A_Q_DIM][a_cols], even_w_out[j][A_Q_DIM:]], axis=0).astype(BF16)
            rope_groups = (A_Q_DIM + A_KV_DIM) // LANES
            proj = _in_proj(h, nw[0:1], w_in, cos_t, sin_t, rope_groups, seq).reshape(-1, b, seq, LANES)
            tq = min(ATTN_A_TQ, seq)
            sinks = jnp.repeat(even_sinks[j][np.array(_A_HEAD_ORDER)].astype(F32) * LOG2E, tq)[None, None, :]
            n_groups = A_Q_DIM // LANES
            mask = jnp.asarray(np.tile(_band_mask(tq, A_PREV_CHUNKS).T, (1, 2 * n_groups))[None, None])
            oa = _band_attention(proj, mask, q_col=0, k_col=n_groups, v_col=n_groups + 1, n_kv=1,
                                 groups_per_kv=n_groups, units_per_step=1, n_prev=A_PREV_CHUNKS, tq=tq,
                                 sinks=sinks)
            c0 = (A_Q_DIM + 2 * A_KV_DIM) // LANES
            ob = _hgrn(proj, hgrn_lb_logits, hgrn_norm_w[j:j + 1],
                       q_col=c0, f_col=c0 + B_HEADS, i_col=c0 + 2 * B_HEADS, g_col=c0 + 3 * B_HEADS, layer_j=j)
            mix = [oa.reshape(-1, b * seq, LANES), ob.reshape(-1, b * seq, LANES)]
        else:
            w_out = odd_w_out[j].astype(BF16)
            if fused_proj is None:
                fused_proj = _in_proj(h, nw[0:1], odd_proj_weight(j), cos_t, sin_t, 0, seq)
            proj = fused_proj.reshape(-1, b, seq, LANES)
            pad = C_PREV_CHUNKS * CHUNK
            n_groups = C_DIM // LANES
            tq = min(ATTN_C_TQ, seq)
            n_qc = tq // CHUNK
            rel_bias = _rel_bias(odd_rel_bias[j].astype(F32) * LOG2E)
            bias = jnp.stack([jnp.pad(rel_bias, ((0, 0), (0, 0), (qc * CHUNK, (n_qc - 1 - qc) * CHUNK)),
                                      constant_values=-jnp.inf) for qc in range(n_qc)], axis=1)
            bias = jnp.swapaxes(bias.reshape(n_groups, 2 * tq, tq + pad), 1, 2)
            bias = bias.reshape(n_groups // ATTN_C_UNITS, ATTN_C_UNITS, tq + pad, 2 * tq)
            oc = _band_attention(proj, bias, q_col=0, k_col=n_groups, v_col=2 * n_groups, n_kv=n_groups,
                                 groups_per_kv=1, units_per_step=ATTN_C_UNITS, n_prev=C_PREV_CHUNKS, tq=tq)
            mix = [oc.reshape(-1, b * seq, LANES)]
        next_proj = None
        if layer + 1 < depth and (layer + 1) % 2 == 1:
            next_proj = (norm_w[layer + 1][0:1], odd_proj_weight((layer + 1) // 2))
        out = _out_ffn(mix, h, w_out, norm_w, w_up_all, ffn_conv_w, conv_b_all, w_down_all, layer, seq,
                       next_proj=next_proj)
        h, fused_proj = out if next_proj is not None else (out, None)
    return h.reshape(b, seq, d)
```
